```python
import math
import jax
import jax.numpy as jnp
from jax import lax
import numpy as np

D_MODEL = 1024
BATCH = 8
SEQ = 2048
DEPTH = 1

DA_HEADS = 4
DA_HEAD_DIM = 64
DA_V_DIM = 2 * DA_HEAD_DIM
RET_HEADS = 4
RET_QK_DIM = 64
RET_V_DIM = 128
RET_CHUNK = 128
Q_BLOCK = 128
D_FF = 2816
ROPE_THETA = 10000.0
EPS = 1e-6
N_MOD = 9

DA_QK_W = DA_HEADS * 2 * DA_HEAD_DIM
DA_V_W = DA_HEADS * DA_V_DIM
RET_QK_W = RET_HEADS * RET_QK_DIM
RET_V_W = RET_HEADS * RET_V_DIM
IN_SPLITS = (DA_QK_W, DA_QK_W, DA_V_W, RET_QK_W, RET_QK_W, RET_V_W, RET_V_W, D_MODEL, D_MODEL)
IN_WIDTH = 2 * DA_QK_W + DA_V_W + 2 * RET_QK_W + 2 * RET_V_W + 2 * D_MODEL

kernel_name = 'hybrid_diffattn_retention_macaron'


def rmsnorm(x, g):
    xf = x.astype(jnp.float32)
    y = xf * lax.rsqrt(jnp.mean(xf * xf, axis=-1, keepdims=True) + EPS)
    return (y * g.astype(jnp.float32)).astype(x.dtype)


def rope_half(x, pos):
    d = x.shape[-1]
    inv = 1.0 / (ROPE_THETA ** (jnp.arange(0, d, 2, dtype=jnp.float32) / d))
    ang = pos.astype(jnp.float32)[..., None] * inv
    cos = jnp.cos(ang)[:, :, None, :]
    sin = jnp.sin(ang)[:, :, None, :]
    xf = x.astype(jnp.float32)
    x1, x2 = xf[..., : d // 2], xf[..., d // 2:]
    return jnp.concatenate([x1 * cos - x2 * sin, x2 * cos + x1 * sin], axis=-1).astype(x.dtype)


def retnet_rotate(x, pos):
    d = x.shape[-1]
    angle = 1.0 / (ROPE_THETA ** jnp.linspace(0.0, 1.0, d // 2, dtype=jnp.float32))
    ang = pos.astype(jnp.float32)[..., None] * angle
    cos = jnp.cos(ang)[:, :, None, :]
    sin = jnp.sin(ang)[:, :, None, :]
    xf = x.astype(jnp.float32)
    xe, xo = xf[..., 0::2], xf[..., 1::2]
    out = jnp.stack([xe * cos - xo * sin, xo * cos + xe * sin], axis=-1)
    return out.reshape(x.shape).astype(x.dtype)


def swiglu(h, w1, w3, w2):
    return (jax.nn.silu(h @ w1) * (h @ w3)) @ w2


def diff_attention(q, k, v, lam):
    b, nh, _, s, d = q.shape
    nq = s // Q_BLOCK
    scale = d ** -0.5
    qb = q.reshape(b, nh, 2, nq, Q_BLOCK, d).transpose(3, 0, 1, 2, 4, 5)

    def block(qblk):
        sc = jnp.einsum('bhmqd,bhmkd->bhmqk', qblk, k).astype(jnp.float32) * scale
        p = jax.nn.softmax(sc, axis=-1)
        w = p[:, :, 0] - lam * p[:, :, 1]
        return jnp.einsum('bhqk,bhkv->bhqv', w.astype(v.dtype), v)

    o = lax.map(block, qb)
    return o.transpose(1, 0, 3, 2, 4).reshape(b, s, nh, v.shape[-1])


def retention_chunkwise(q, k, v, log_gamma):
    b, nh, s, dk = q.shape
    dv = v.shape[-1]
    n = s // RET_CHUNK
    qc = q.reshape(b, nh, n, RET_CHUNK, dk)
    kc = k.reshape(b, nh, n, RET_CHUNK, dk)
    vc = v.reshape(b, nh, n, RET_CHUNK, dv)
    idx = jnp.arange(RET_CHUNK, dtype=jnp.float32)
    rel = idx[:, None] - idx[None, :]
    lower = rel >= 0
    lg = log_gamma[:, None, None]
    dmat = jnp.where(lower[None], jnp.exp(jnp.where(lower, rel, 0.0)[None] * lg), 0.0)
    scores = jnp.einsum('bhncd,bhnjd->bhncj', qc, kc) * dmat[None, :, None]
    intra = jnp.einsum('bhncj,bhnje->bhnce', scores, vc)
    zeta = jnp.exp((RET_CHUNK - 1.0 - idx)[None, :] * log_gamma[:, None])
    xi = jnp.exp((idx + 1.0)[None, :] * log_gamma[:, None])
    kv = jnp.einsum('bhnjd,bhnje->nbhde', kc * zeta[None, :, None, :, None], vc)
    chunk_decay = jnp.exp(RET_CHUNK * log_gamma)[None, :, None, None]

    def step(state, kv_n):
        return state * chunk_decay + kv_n, state

    _, r_prev = lax.scan(step, jnp.zeros((b, nh, dk, dv), jnp.float32), kv)
    cross = jnp.einsum('bhncd,nbhde->bhnce', qc * xi[None, :, None, :, None], r_prev)
    return (intra + cross).reshape(b, nh, s, dv)


def token_mixing(h, positions, w_in, b_merge, da_q_gain, da_k_gain, lq1, lk1, lq2, lk2,
                 da_subln, ret_decay_f, ret_decay_b, ret_norm, w_branch_a, w_branch_r,
                 w_out, lam_init):
    b, s, _ = h.shape
    z = h @ w_in
    split_idx = [int(i) for i in np.cumsum(IN_SPLITS)[:-1]]
    qa, ka, va, qr, kr, vr, ret_gate, merge_a, merge_r = jnp.split(z, split_idx, axis=-1)

    qa = rope_half(rmsnorm(qa.reshape(b, s, DA_HEADS * 2, DA_HEAD_DIM), da_q_gain), positions)
    ka = rope_half(rmsnorm(ka.reshape(b, s, DA_HEADS * 2, DA_HEAD_DIM), da_k_gain), positions)
    qa = qa.reshape(b, s, DA_HEADS, 2, DA_HEAD_DIM).transpose(0, 2, 3, 1, 4)
    ka = ka.reshape(b, s, DA_HEADS, 2, DA_HEAD_DIM).transpose(0, 2, 3, 1, 4)
    va = va.reshape(b, s, DA_HEADS, DA_V_DIM).transpose(0, 2, 1, 3)
    f32 = jnp.float32
    lam = (jnp.exp(jnp.sum(lq1.astype(f32) * lk1.astype(f32)))
           - jnp.exp(jnp.sum(lq2.astype(f32) * lk2.astype(f32))) + lam_init)
    oa = diff_attention(qa, ka, va, lam)
    oa = (rmsnorm(oa, da_subln) * (1.0 - lam_init)).reshape(b, s, DA_V_W)

    qr = retnet_rotate(qr.reshape(b, s, RET_HEADS, RET_QK_DIM), positions)
    kr = retnet_rotate(kr.reshape(b, s, RET_HEADS, RET_QK_DIM), positions) * (RET_QK_DIM ** -0.5)
    qr = qr.astype(f32).transpose(0, 2, 1, 3)
    kr = kr.astype(f32).transpose(0, 2, 1, 3)
    vr = vr.reshape(b, s, RET_HEADS, RET_V_DIM).astype(f32).transpose(0, 2, 1, 3)
    lg_f = jax.nn.log_sigmoid(ret_decay_f.astype(f32))
    lg_b = jax.nn.log_sigmoid(ret_decay_b.astype(f32))
    y_f = retention_chunkwise(qr, kr, vr, lg_f)
    y_b = jnp.flip(retention_chunkwise(jnp.flip(qr, 2), jnp.flip(kr, 2), jnp.flip(vr, 2), lg_b), 2)
    y = (y_f + y_b).transpose(0, 2, 1, 3).astype(h.dtype)
    y = rmsnorm(y, ret_norm).reshape(b, s, RET_V_W) * jax.nn.silu(ret_gate)

    p_a = oa @ w_branch_a
    p_r = y @ w_branch_r
    merged = (jax.nn.sigmoid(merge_a + b_merge[0]) * p_a
              + jax.nn.sigmoid(merge_r + b_merge[1]) * p_r)
    return merged @ w_out


def setup_inputs(seed: int = 0) -> dict:
    key = jax.random.key(seed)
    ks = jax.random.split(key, 32)
    f32 = jnp.float32
    L = DEPTH
    D = D_MODEL

    def nrm(k, shape, scale):
        return jax.random.normal(k, shape, f32) * scale

    def gain(k, shape):
        return 1.0 + 0.05 * jax.random.normal(k, shape, f32)

    base_logit = jnp.log(2.0 ** (5.0 + jnp.arange(RET_HEADS, dtype=f32)) - 1.0)
    offset = jax.random.randint(ks[2], (BATCH, 1), 0, 1024, dtype=jnp.int32)
    positions = jnp.arange(SEQ, dtype=jnp.int32)[None, :] + offset
    return {
        'x': nrm(ks[0], (BATCH, SEQ, D), 1.0),
        'c': nrm(ks[1], (BATCH, D), 1.0),
        'positions': positions,
        'w_ada': nrm(ks[3], (L, D, N_MOD * D), D ** -0.5),
        'b_ada': nrm(ks[4], (L, N_MOD * D), 0.02),
        'norm_ffn1': gain(ks[5], (L, D)),
        'ffn1_w1': nrm(ks[6], (L, D, D_FF), D ** -0.5),
        'ffn1_w3': nrm(ks[7], (L, D, D_FF), D ** -0.5),
        'ffn1_w2': nrm(ks[8], (L, D_FF, D), D_FF ** -0.5),
        'norm_mix': gain(ks[9], (L, D)),
        'w_in': nrm(ks[10], (L, D, IN_WIDTH), D ** -0.5),
        'b_merge': nrm(ks[11], (L, 2, D), 0.02),
        'da_q_gain': gain(ks[12], (L, DA_HEAD_DIM)),
        'da_k_gain': gain(ks[13], (L, DA_HEAD_DIM)),
        'da_lambda_q1': nrm(ks[14], (L, DA_HEAD_DIM), 0.1),
        'da_lambda_k1': nrm(ks[15], (L, DA_HEAD_DIM), 0.1),
        'da_lambda_q2': nrm(ks[16], (L, DA_HEAD_DIM), 0.1),
        'da_lambda_k2': nrm(ks[17], (L, DA_HEAD_DIM), 0.1),
        'da_subln': gain(ks[18], (L, DA_V_DIM)),
        'ret_decay_f': base_logit[None] + nrm(ks[19], (L, RET_HEADS), 0.1),
        'ret_decay_b': base_logit[None] + nrm(ks[20], (L, RET_HEADS), 0.1),
        'ret_norm': gain(ks[21], (L, RET_V_DIM)),
        'w_branch_a': nrm(ks[22], (L, DA_V_W, D), DA_V_W ** -0.5),
        'w_branch_r': nrm(ks[23], (L, RET_V_W, D), RET_V_W ** -0.5),
        'w_out': nrm(ks[24], (L, D, D), D ** -0.5),
        'norm_ffn2': gain(ks[25], (L, D)),
        'ffn2_w1': nrm(ks[26], (L, D, D_FF), D ** -0.5),
        'ffn2_w3': nrm(ks[27], (L, D, D_FF), D ** -0.5),
        'ffn2_w2': nrm(ks[28], (L, D_FF, D), D_FF ** -0.5),
    }


def reference(x, c, positions, w_ada, b_ada, norm_ffn1, ffn1_w1, ffn1_w3, ffn1_w2,
              norm_mix, w_in, b_merge, da_q_gain, da_k_gain, da_lambda_q1, da_lambda_k1,
              da_lambda_q2, da_lambda_k2, da_subln, ret_decay_f, ret_decay_b, ret_norm,
              w_branch_a, w_branch_r, w_out, norm_ffn2, ffn2_w1, ffn2_w3, ffn2_w2):
    b = x.shape[0]
    for l in range(DEPTH):
        lam_init = 0.8 - 0.6 * math.exp(-0.3 * l)
        mod = (jax.nn.silu(c) @ w_ada[l] + b_ada[l]).reshape(b, N_MOD, 1, D_MODEL)
        sh1, sc1, g1, sh2, sc2, g2, sh3, sc3, g3 = [mod[:, i] for i in range(N_MOD)]

        h = rmsnorm(x, norm_ffn1[l]) * (1.0 + sc1) + sh1
        x = x + 0.5 * g1 * swiglu(h, ffn1_w1[l], ffn1_w3[l], ffn1_w2[l])

        h = rmsnorm(x, norm_mix[l]) * (1.0 + sc2) + sh2
        x = x + g2 * token_mixing(h, positions, w_in[l], b_merge[l], da_q_gain[l], da_k_gain[l],
                                  da_lambda_q1[l], da_lambda_k1[l], da_lambda_q2[l], da_lambda_k2[l],
                                  da_subln[l], ret_decay_f[l], ret_decay_b[l], ret_norm[l],
                                  w_branch_a[l], w_branch_r[l], w_out[l], lam_init)

        h = rmsnorm(x, norm_ffn2[l]) * (1.0 + sc3) + sh3
        x = x + 0.5 * g3 * swiglu(h, ffn2_w1[l], ffn2_w3[l], ffn2_w2[l])
    return x
```

```python
import functools
import math

import jax
import jax.numpy as jnp
from jax import lax
from jax.experimental import pallas as pl
from jax.experimental.pallas import tpu as pltpu

F32 = jnp.float32
BF16 = jnp.bfloat16

DA_HEADS = 4
DA_HEAD_DIM = 64
DA_V_DIM = 2 * DA_HEAD_DIM
RET_HEADS = 4
RET_QK_DIM = 64
RET_V_DIM = 128
ROPE_THETA = 10000.0
EPS = 1e-6
N_MOD = 9

DA_QK_W = DA_HEADS * 2 * DA_HEAD_DIM
DA_V_W = DA_HEADS * DA_V_DIM
RET_QK_W = RET_HEADS * RET_QK_DIM
RET_V_W = RET_HEADS * RET_V_DIM

LANES = 128
VMEM_LIMIT_BYTES = 56 * 1024 * 1024


def _cparams(n_axes):
    return pltpu.CompilerParams(
        dimension_semantics=("arbitrary",) * n_axes,
        vmem_limit_bytes=VMEM_LIMIT_BYTES,
    )


def _resident(shape):
    nd = len(shape)
    return pl.BlockSpec(shape, lambda *_: (0,) * nd, pipeline_mode=pl.Buffered(1))


def _silu(a):
    return a * jax.nn.sigmoid(a)


def _modnorm(x, g, sc, sh):
    ms = jnp.mean(x * x, axis=-1, keepdims=True)
    y = x * lax.rsqrt(ms + EPS) * g
    return y * (1.0 + sc) + sh


def _mod_rows(mod_ref, k0):
    return (mod_ref[0, k0:k0 + 1, :], mod_ref[0, k0 + 1:k0 + 2, :], mod_ref[0, k0 + 2:k0 + 3, :])


def _ada_kernel(c_ref, w_ref, b_ref, o_ref):
    o_ref[...] = jnp.dot(_silu(c_ref[...]), w_ref[...], preferred_element_type=F32) + b_ref[...]


def _ada(c, w, b, tn=1024):
    bsz, d = c.shape
    n = w.shape[1]
    return pl.pallas_call(
        _ada_kernel,
        out_shape=jax.ShapeDtypeStruct((bsz, n), F32),
        grid=(n // tn,),
        in_specs=[
            pl.BlockSpec((bsz, d), lambda j: (0, 0)),
            pl.BlockSpec((d, tn), lambda j: (0, j)),
            pl.BlockSpec((1, tn), lambda j: (0, j)),
        ],
        out_specs=pl.BlockSpec((bsz, tn), lambda j: (0, j)),
        compiler_params=_cparams(1),
        name="ada_mod",
    )(c, w, b)


def _ffn_kernel(x_ref, mod_ref, g_ref, w1_ref, w3_ref, w2_ref, o_ref, *, k0, nchunk):
    x = x_ref[...]
    sh, sc, gt = _mod_rows(mod_ref, k0)
    h = _modnorm(x, g_ref[...], sc, sh).astype(BF16)
    fc = w1_ref.shape[1] // nchunk
    acc = None
    for c in range(nchunk):
        a = jnp.dot(h, w1_ref[:, c * fc:(c + 1) * fc], preferred_element_type=F32)
        b = jnp.dot(h, w3_ref[:, c * fc:(c + 1) * fc], preferred_element_type=F32)
        u = (_silu(a) * b).astype(BF16)
        d = jnp.dot(u, w2_ref[c * fc:(c + 1) * fc, :], preferred_element_type=F32)
        acc = d if acc is None else acc + d
    o_ref[...] = x + (0.5 * gt) * acc


def _ffn(x, mod, g, w1, w3, w2, *, k0, seq, tm=512, nchunk=2):
    t, d = x.shape
    f = w1.shape[1]
    return pl.pallas_call(
        functools.partial(_ffn_kernel, k0=k0, nchunk=nchunk),
        out_shape=jax.ShapeDtypeStruct((t, d), F32),
        grid=(t // tm,),
        in_specs=[
            pl.BlockSpec((tm, d), lambda i: (i, 0)),
            pl.BlockSpec((1, N_MOD, d), lambda i: (i * tm // seq, 0, 0)),
            _resident((1, d)),
            _resident((d, f)),
            _resident((d, f)),
            _resident((f, d)),
        ],
        out_specs=pl.BlockSpec((tm, d), lambda i: (i, 0)),
        compiler_params=_cparams(1),
        name=f"ffn_{k0}",
    )(x, mod, g, w1, w3, w2)


def _rot_tables(pos, inv):
    ang = pos * inv
    c = jnp.cos(ang)
    s = jnp.sin(ang)
    lane = lax.broadcasted_iota(jnp.int32, (1, LANES), 1)
    first = (lane & 32) == 0
    return c, jnp.where(first, -s, 0.0), jnp.where(first, 0.0, s)


def _rot128(y, tabs):
    c, sa, sb = tabs
    return y * c + pltpu.roll(y, 96, 1) * sa + pltpu.roll(y, 32, 1) * sb


def _proj_kernel(x_ref, mod_ref, g_ref, pos_ref, w_ref, gq_ref, gk_ref, gsum_ref, inv_a_ref, inv_r_ref,
                 qa_ref, ka_ref, va_ref, qr_ref, kr_ref, vr_ref, gate_ref):
    x = x_ref[...]
    sh, sc, _ = _mod_rows(mod_ref, 3)
    h = _modnorm(x, g_ref[...], sc, sh).astype(BF16)
    z = jnp.dot(h, w_ref[...], preferred_element_type=F32)
    pos = pos_ref[...].astype(F32)
    tab_a = _rot_tables(pos, inv_a_ref[...])
    tab_r = _rot_tables(pos, inv_r_ref[...])

    def qk_norm_rot(zs, gain, out_ref):
        ss = jnp.dot((zs * zs).astype(BF16), gsum_ref[...], preferred_element_type=F32)
        y = zs * lax.rsqrt(ss * (1.0 / DA_HEAD_DIM) + EPS) * gain
        for j in range(DA_QK_W // LANES):
            sl = slice(j * LANES, (j + 1) * LANES)
            out_ref[:, sl] = _rot128(y[:, sl], tab_a).astype(out_ref.dtype)

    o = 0
    qk_norm_rot(z[:, o:o + DA_QK_W], gq_ref[...] * (DA_HEAD_DIM ** -0.5), qa_ref)
    o += DA_QK_W
    qk_norm_rot(z[:, o:o + DA_QK_W], gk_ref[...], ka_ref)
    o += DA_QK_W
    va_ref[...] = z[:, o:o + DA_V_W].astype(va_ref.dtype)
    o += DA_V_W
    for j in range(RET_QK_W // LANES):
        qr_ref[:, j * LANES:(j + 1) * LANES] = _rot128(
            z[:, o + j * LANES:o + (j + 1) * LANES], tab_r).astype(qr_ref.dtype)
    o += RET_QK_W
    for j in range(RET_QK_W // LANES):
        kr_ref[:, j * LANES:(j + 1) * LANES] = (_rot128(
            z[:, o + j * LANES:o + (j + 1) * LANES], tab_r) * (RET_QK_DIM ** -0.5)).astype(kr_ref.dtype)
    o += RET_QK_W
    vr_ref[...] = z[:, o:o + RET_V_W].astype(vr_ref.dtype)
    o += RET_V_W
    gate_ref[...] = z[:, o:o + RET_V_W].astype(gate_ref.dtype)


def _proj(x, mod, g, pos, w, gq, gk, gsum, inv_a, inv_r, *, seq, tm=512):
    t, d = x.shape
    n = w.shape[1]
    widths = (DA_QK_W, DA_QK_W, DA_V_W, RET_QK_W, RET_QK_W, RET_V_W, RET_V_W)
    return pl.pallas_call(
        _proj_kernel,
        out_shape=[jax.ShapeDtypeStruct((t, wd), BF16) for wd in widths],
        grid=(t // tm,),
        in_specs=[
            pl.BlockSpec((tm, d), lambda i: (i, 0)),
            pl.BlockSpec((1, N_MOD, d), lambda i: (i * tm // seq, 0, 0)),
            _resident((1, d)),
            pl.BlockSpec((tm, 1), lambda i: (i, 0)),
            _resident((d, n)),
            _resident((1, DA_QK_W)),
            _resident((1, DA_QK_W)),
            _resident((DA_QK_W, DA_QK_W)),
            _resident((1, LANES)),
            _resident((1, LANES)),
        ],
        out_specs=[pl.BlockSpec((tm, wd), lambda i: (i, 0)) for wd in widths],
        compiler_params=_cparams(1),
        name="mix_proj",
    )(x, mod, g, pos, w, gq, gk, gsum, inv_a, inv_r)


def _dot_nt(a, b):
    return lax.dot_general(a, b, (((1,), (1,)), ((), ())), preferred_element_type=F32)


def _softmax_parts(s):
    m = jnp.max(s, axis=-1, keepdims=True)
    p = jnp.exp(s - m)
    return p, jnp.sum(p, axis=-1, keepdims=True)


def _dattn_kernel(q_ref, k_ref, v_ref, lq1_ref, lk1_ref, lq2_ref, lk2_ref, gs_ref, o_ref, *, lam_init):
    q = q_ref[...]
    k = k_ref[...]
    lane = lax.broadcasted_iota(jnp.int32, q.shape, 1)
    zero = jnp.zeros_like(q)
    p1, l1 = _softmax_parts(_dot_nt(jnp.where(lane < DA_HEAD_DIM, q, zero), k))
    p2, l2 = _softmax_parts(_dot_nt(jnp.where(lane >= DA_HEAD_DIM, q, zero), k))
    lam = (jnp.exp(jnp.sum(lq1_ref[...] * lk1_ref[...], axis=-1, keepdims=True))
           - jnp.exp(jnp.sum(lq2_ref[...] * lk2_ref[...], axis=-1, keepdims=True)) + lam_init)
    w = p1 * (1.0 / l1) - p2 * (lam / l2)
    o = jnp.dot(w.astype(BF16), v_ref[...], preferred_element_type=F32)
    ms = jnp.mean(o * o, axis=-1, keepdims=True)
    o_ref[...] = (o * lax.rsqrt(ms + EPS) * gs_ref[...] * (1.0 - lam_init)).astype(o_ref.dtype)


def _dattn(qa, ka, va, lq1, lk1, lq2, lk2, gs, *, bsz, seq, lam_init, tq=256):
    t = qa.shape[0]
    nq = seq // tq
    lam_spec = _resident((1, DA_HEAD_DIM))
    return pl.pallas_call(
        functools.partial(_dattn_kernel, lam_init=lam_init),
        out_shape=jax.ShapeDtypeStruct((t, DA_V_W), BF16),
        grid=(bsz, DA_HEADS, nq),
        in_specs=[
            pl.BlockSpec((tq, LANES), lambda b, h, i: (b * nq + i, h)),
            pl.BlockSpec((seq, LANES), lambda b, h, i: (b, h)),
            pl.BlockSpec((seq, LANES), lambda b, h, i: (b, h)),
            lam_spec, lam_spec, lam_spec, lam_spec,
            _resident((1, DA_V_DIM)),
        ],
        out_specs=pl.BlockSpec((tq, LANES), lambda b, h, i: (b * nq + i, h)),
        compiler_params=_cparams(3),
        name="diff_attn",
    )(qa, ka, va, lq1, lk1, lq2, lk2, gs)


def _ret_kernel(q_ref, k_ref, v_ref, gate_ref, df_ref, db_ref, gn_ref, o_ref, *, tq):
    hh = pl.program_id(1)
    qi = pl.program_id(2)
    q = q_ref[...]
    lane = lax.broadcasted_iota(jnp.int32, q.shape, 1)
    mine = (lane // RET_QK_DIM) == (hh % 2)
    s = _dot_nt(jnp.where(mine, q, jnp.zeros_like(q)), k_ref[...])

    hl = lax.broadcasted_iota(jnp.int32, df_ref.shape, 1) == hh

    def head_log_gamma(ref):
        lg = jax.nn.log_sigmoid(ref[...])
        return jnp.sum(jnp.where(hl, lg, 0.0), axis=-1, keepdims=True)

    lgf = head_log_gamma(df_ref)
    lgb = head_log_gamma(db_ref)
    i = lax.broadcasted_iota(jnp.int32, s.shape, 0) + qi * tq
    j = lax.broadcasted_iota(jnp.int32, s.shape, 1)
    rel = (i - j).astype(F32)
    dec = jnp.exp(jnp.where(rel >= 0, lgf, -lgb) * rel)
    dec = jnp.where(rel == 0, 2.0, dec)
    y = jnp.dot((s * dec).astype(BF16), v_ref[...], preferred_element_type=F32)
    ms = jnp.mean(y * y, axis=-1, keepdims=True)
    y = y * lax.rsqrt(ms + EPS) * gn_ref[...]
    o_ref[...] = (y * _silu(gate_ref[...].astype(F32))).astype(o_ref.dtype)


def _retention(qr, kr, vr, gate, df, db, gn, *, bsz, seq, tq=256):
    t = qr.shape[0]
    nq = seq // tq
    return pl.pallas_call(
        functools.partial(_ret_kernel, tq=tq),
        out_shape=jax.ShapeDtypeStruct((t, RET_V_W), BF16),
        grid=(bsz, RET_HEADS, nq),
        in_specs=[
            pl.BlockSpec((tq, LANES), lambda b, h, i: (b * nq + i, h // 2)),
            pl.BlockSpec((seq, LANES), lambda b, h, i: (b, h // 2)),
            pl.BlockSpec((seq, LANES), lambda b, h, i: (b, h)),
            pl.BlockSpec((tq, LANES), lambda b, h, i: (b * nq + i, h)),
            _resident((1, RET_HEADS)),
            _resident((1, RET_HEADS)),
            _resident((1, RET_V_DIM)),
        ],
        out_specs=pl.BlockSpec((tq, LANES), lambda b, h, i: (b * nq + i, h)),
        compiler_params=_cparams(3),
        name="retention",
    )(qr, kr, vr, gate, df, db, gn)


def _merge_kernel(x_ref, mod_ref, g_ref, oa_ref, yr_ref, wm_ref, bm_ref, wa_ref, wr_ref, wo_ref, o_ref):
    x = x_ref[...]
    d = x.shape[1]
    sh, sc, gt = _mod_rows(mod_ref, 3)
    h = _modnorm(x, g_ref[...], sc, sh).astype(BF16)
    ml = jnp.dot(h, wm_ref[...], preferred_element_type=F32)
    pa = jnp.dot(oa_ref[...], wa_ref[...], preferred_element_type=F32)
    pr = jnp.dot(yr_ref[...], wr_ref[...], preferred_element_type=F32)
    merged = (jax.nn.sigmoid(ml[:, :d] + bm_ref[0:1, :]) * pa
              + jax.nn.sigmoid(ml[:, d:] + bm_ref[1:2, :]) * pr)
    out = jnp.dot(merged.astype(BF16), wo_ref[...], preferred_element_type=F32)
    o_ref[...] = x + gt * out


def _merge(x, mod, g, oa, yr, wm, bm, wa, wr, wo, *, seq, tm=512):
    t, d = x.shape
    return pl.pallas_call(
        _merge_kernel,
        out_shape=jax.ShapeDtypeStruct((t, d), F32),
        grid=(t // tm,),
        in_specs=[
            pl.BlockSpec((tm, d), lambda i: (i, 0)),
            pl.BlockSpec((1, N_MOD, d), lambda i: (i * tm // seq, 0, 0)),
            _resident((1, d)),
            pl.BlockSpec((tm, DA_V_W), lambda i: (i, 0)),
            pl.BlockSpec((tm, RET_V_W), lambda i: (i, 0)),
            _resident((d, 2 * d)),
            _resident((2, d)),
            _resident((DA_V_W, d)),
            _resident((RET_V_W, d)),
            _resident((d, d)),
        ],
        out_specs=pl.BlockSpec((tm, d), lambda i: (i, 0)),
        compiler_params=_cparams(1),
        name="mix_merge",
    )(x, mod, g, oa, yr, wm, bm, wa, wr, wo)


def _deinterleave_heads(w, heads, dim):
    d = w.shape[0]
    return w.reshape(d, heads, dim // 2, 2).transpose(0, 1, 3, 2).reshape(d, heads * dim)


def _lane_freqs(inv_half):
    return jnp.tile(inv_half, LANES // inv_half.shape[0]).reshape(1, LANES)


def kernel(x, c, positions, w_ada, b_ada, norm_ffn1, ffn1_w1, ffn1_w3, ffn1_w2, norm_mix, w_in, b_merge,
           da_q_gain, da_k_gain, da_lambda_q1, da_lambda_k1, da_lambda_q2, da_lambda_k2, da_subln,
           ret_decay_f, ret_decay_b, ret_norm, w_branch_a, w_branch_r, w_out, norm_ffn2, ffn2_w1, ffn2_w3,
           ffn2_w2):
    bsz, seq, d = x.shape
    depth = w_ada.shape[0]
    t = bsz * seq
    xt = x.reshape(t, d)
    pos = positions.reshape(t, 1)

    inv_a = _lane_freqs(1.0 / (ROPE_THETA ** (jnp.arange(0, DA_HEAD_DIM, 2, dtype=F32) / DA_HEAD_DIM)))
    inv_r = _lane_freqs(1.0 / (ROPE_THETA ** jnp.linspace(0.0, 1.0, RET_QK_DIM // 2, dtype=F32)))
    n_groups = DA_QK_W // DA_HEAD_DIM
    gsum = jnp.kron(jnp.eye(n_groups, dtype=F32), jnp.ones((DA_HEAD_DIM, DA_HEAD_DIM), F32)).astype(BF16)

    o_qr = 2 * DA_QK_W + DA_V_W
    o_kr = o_qr + RET_QK_W
    o_vr = o_kr + RET_QK_W
    o_merge = o_vr + 2 * RET_V_W

    for l in range(depth):
        lam_init = 0.8 - 0.6 * math.exp(-0.3 * l)
        mod = _ada(c, w_ada[l], b_ada[l].reshape(1, -1)).reshape(bsz, N_MOD, d)

        xt = _ffn(xt, mod, norm_ffn1[l].reshape(1, d), ffn1_w1[l].astype(BF16), ffn1_w3[l].astype(BF16),
                  ffn1_w2[l].astype(BF16), k0=0, seq=seq)

        wl = w_in[l]
        w_proj = jnp.concatenate([
            wl[:, :o_qr],
            _deinterleave_heads(wl[:, o_qr:o_kr], RET_HEADS, RET_QK_DIM),
            _deinterleave_heads(wl[:, o_kr:o_vr], RET_HEADS, RET_QK_DIM),
            wl[:, o_vr:o_merge]], axis=1).astype(BF16)
        qa, ka, va, qr, kr, vr, gate = _proj(
            xt, mod, norm_mix[l].reshape(1, d), pos, w_proj,
            jnp.tile(da_q_gain[l], n_groups).reshape(1, DA_QK_W),
            jnp.tile(da_k_gain[l], n_groups).reshape(1, DA_QK_W),
            gsum, inv_a, inv_r, seq=seq)

        oa = _dattn(qa, ka, va, da_lambda_q1[l].reshape(1, -1), da_lambda_k1[l].reshape(1, -1),
                    da_lambda_q2[l].reshape(1, -1), da_lambda_k2[l].reshape(1, -1),
                    da_subln[l].reshape(1, -1), bsz=bsz, seq=seq, lam_init=lam_init)
        yr = _retention(qr, kr, vr, gate, ret_decay_f[l].reshape(1, -1), ret_decay_b[l].reshape(1, -1),
                        ret_norm[l].reshape(1, -1), bsz=bsz, seq=seq)

        xt = _merge(xt, mod, norm_mix[l].reshape(1, d), oa, yr, wl[:, o_merge:].astype(BF16), b_merge[l],
                    w_branch_a[l].astype(BF16), w_branch_r[l].astype(BF16), w_out[l].astype(BF16), seq=seq)

        xt = _ffn(xt, mod, norm_ffn2[l].reshape(1, d), ffn2_w1[l].astype(BF16), ffn2_w3[l].astype(BF16),
                  ffn2_w2[l].astype(BF16), k0=6, seq=seq)
    return xt.reshape(bsz, seq, d)
```

```python
import functools
import math

import jax
import jax.numpy as jnp
from jax import lax
from jax.experimental import pallas as pl
from jax.experimental.pallas import tpu as pltpu

F32 = jnp.float32
BF16 = jnp.bfloat16

DA_HEADS = 4
DA_HEAD_DIM = 64
DA_V_DIM = 2 * DA_HEAD_DIM
RET_HEADS = 4
RET_QK_DIM = 64
RET_V_DIM = 128
ROPE_THETA = 10000.0
EPS = 1e-6
N_MOD = 9
LOG2E = math.log2(math.e)

DA_QK_W = DA_HEADS * 2 * DA_HEAD_DIM
DA_V_W = DA_HEADS * DA_V_DIM
RET_QK_W = RET_HEADS * RET_QK_DIM
RET_V_W = RET_HEADS * RET_V_DIM

LANES = 128
VMEM_LIMIT_BYTES = 56 * 1024 * 1024


def _cparams(n_axes):
    return pltpu.CompilerParams(
        dimension_semantics=("arbitrary",) * n_axes,
        vmem_limit_bytes=VMEM_LIMIT_BYTES,
    )


def _resident(shape):
    nd = len(shape)
    return pl.BlockSpec(shape, lambda *_: (0,) * nd, pipeline_mode=pl.Buffered(1))


def _silu(a):
    return a * jax.nn.sigmoid(a)


def _modnorm(x, g, sc, sh):
    ms = jnp.mean(x * x, axis=-1, keepdims=True)
    y = x * lax.rsqrt(ms + EPS) * g
    return y * (1.0 + sc) + sh


def _mod_rows(mod_ref, k0):
    return (mod_ref[0, k0:k0 + 1, :], mod_ref[0, k0 + 1:k0 + 2, :], mod_ref[0, k0 + 2:k0 + 3, :])


def _dot_nt(a, b):
    return lax.dot_general(a, b, (((1,), (1,)), ((), ())), preferred_element_type=F32)


def _ada_kernel(c_ref, w_ref, b_ref, o_ref):
    o_ref[...] = jnp.dot(_silu(c_ref[...]), w_ref[...], preferred_element_type=F32) + b_ref[...]


def _ada(c, w, b, tn=1024):
    bsz, d = c.shape
    n = w.shape[1]
    return pl.pallas_call(
        _ada_kernel,
        out_shape=jax.ShapeDtypeStruct((bsz, n), F32),
        grid=(n // tn,),
        in_specs=[
            pl.BlockSpec((bsz, d), lambda j: (0, 0)),
            pl.BlockSpec((d, tn), lambda j: (0, j)),
            pl.BlockSpec((1, tn), lambda j: (0, j)),
        ],
        out_specs=pl.BlockSpec((bsz, tn), lambda j: (0, j)),
        compiler_params=_cparams(1),
        name="ada_mod",
    )(c, w, b)


def _ffn_kernel(x_ref, mod_ref, g_ref, w1_ref, w3_ref, w2_ref, o_ref, *, k0, nchunk):
    x = x_ref[...]
    sh, sc, gt = _mod_rows(mod_ref, k0)
    h = _modnorm(x, g_ref[...], sc, sh).astype(BF16)
    fc = w1_ref.shape[1] // nchunk
    acc = None
    for c in range(nchunk):
        a = jnp.dot(h, w1_ref[:, c * fc:(c + 1) * fc], preferred_element_type=F32)
        b = jnp.dot(h, w3_ref[:, c * fc:(c + 1) * fc], preferred_element_type=F32)
        u = (_silu(a) * b).astype(BF16)
        d = jnp.dot(u, w2_ref[c * fc:(c + 1) * fc, :], preferred_element_type=F32)
        acc = d if acc is None else acc + d
    o_ref[...] = x + (0.5 * gt) * acc


def _ffn(x, mod, g, w1, w3, w2, *, k0, seq, tm=512, nchunk=2):
    t, d = x.shape
    f = w1.shape[1]
    return pl.pallas_call(
        functools.partial(_ffn_kernel, k0=k0, nchunk=nchunk),
        out_shape=jax.ShapeDtypeStruct((t, d), F32),
        grid=(t // tm,),
        in_specs=[
            pl.BlockSpec((tm, d), lambda i: (i, 0)),
            pl.BlockSpec((1, N_MOD, d), lambda i: (i * tm // seq, 0, 0)),
            _resident((1, d)),
            _resident((d, f)),
            _resident((d, f)),
            _resident((f, d)),
        ],
        out_specs=pl.BlockSpec((tm, d), lambda i: (i, 0)),
        compiler_params=_cparams(1),
        name=f"ffn_{k0}",
    )(x, mod, g, w1, w3, w2)


def _rot_tables(pos, inv):
    ang = pos * inv
    c = jnp.cos(ang)
    s = jnp.sin(ang)
    lane = lax.broadcasted_iota(jnp.int32, (1, LANES), 1)
    first = (lane & 32) == 0
    return c, jnp.where(first, -s, 0.0), jnp.where(first, 0.0, s)


def _rot128(y, tabs):
    c, sa, sb = tabs
    return y * c + pltpu.roll(y, 96, 1) * sa + pltpu.roll(y, 32, 1) * sb


def _proj_kernel(x_ref, mod_ref, g_ref, pos_ref, w_ref, gq_ref, gk_ref, gsum_ref, inv_a_ref, inv_r_ref,
                 qa_ref, ka_ref, vat_ref, qr_ref, kr_ref, krt_ref, vr_ref, gate_ref):
    x = x_ref[...]
    sh, sc, _ = _mod_rows(mod_ref, 3)
    h = _modnorm(x, g_ref[...], sc, sh).astype(BF16)
    z = jnp.dot(h, w_ref[...], preferred_element_type=F32)
    pos = pos_ref[...].astype(F32)
    tab_a = _rot_tables(pos, inv_a_ref[...])
    tab_r = _rot_tables(pos, inv_r_ref[...])

    def qk_norm_rot(zs, gain, out_ref):
        ss = jnp.dot((zs * zs).astype(BF16), gsum_ref[...], preferred_element_type=F32)
        y = zs * lax.rsqrt(ss * (1.0 / DA_HEAD_DIM) + EPS) * gain
        for j in range(DA_QK_W // LANES):
            sl = slice(j * LANES, (j + 1) * LANES)
            out_ref[:, sl] = _rot128(y[:, sl], tab_a).astype(out_ref.dtype)

    o = 0
    qk_norm_rot(z[:, o:o + DA_QK_W], gq_ref[...] * (DA_HEAD_DIM ** -0.5 * LOG2E), qa_ref)
    o += DA_QK_W
    qk_norm_rot(z[:, o:o + DA_QK_W], gk_ref[...], ka_ref)
    o += DA_QK_W
    vat_ref[...] = z[:, o:o + DA_V_W].T.astype(vat_ref.dtype)
    o += DA_V_W
    for j in range(RET_QK_W // LANES):
        qr_ref[:, j * LANES:(j + 1) * LANES] = _rot128(
            z[:, o + j * LANES:o + (j + 1) * LANES], tab_r).astype(qr_ref.dtype)
    o += RET_QK_W
    for j in range(RET_QK_W // LANES):
        kj = _rot128(z[:, o + j * LANES:o + (j + 1) * LANES], tab_r) * (RET_QK_DIM ** -0.5)
        kr_ref[:, j * LANES:(j + 1) * LANES] = kj.astype(kr_ref.dtype)
        krt_ref[j * LANES:(j + 1) * LANES, :] = kj.T.astype(krt_ref.dtype)
    o += RET_QK_W
    vr_ref[...] = z[:, o:o + RET_V_W].astype(vr_ref.dtype)
    o += RET_V_W
    gate_ref[...] = z[:, o:o + RET_V_W].astype(gate_ref.dtype)


def _proj(x, mod, g, pos, w, gq, gk, gsum, inv_a, inv_r, *, seq, tm=512):
    t, d = x.shape
    n = w.shape[1]

    def rows(width):
        return jax.ShapeDtypeStruct((t, width), BF16), pl.BlockSpec((tm, width), lambda i: (i, 0))

    def cols(width):
        return jax.ShapeDtypeStruct((width, t), BF16), pl.BlockSpec((width, tm), lambda i: (0, i))

    outs = [rows(DA_QK_W), rows(DA_QK_W), cols(DA_V_W), rows(RET_QK_W), rows(RET_QK_W), cols(RET_QK_W),
            rows(RET_V_W), rows(RET_V_W)]
    return pl.pallas_call(
        _proj_kernel,
        out_shape=[s for s, _ in outs],
        grid=(t // tm,),
        in_specs=[
            pl.BlockSpec((tm, d), lambda i: (i, 0)),
            pl.BlockSpec((1, N_MOD, d), lambda i: (i * tm // seq, 0, 0)),
            _resident((1, d)),
            pl.BlockSpec((tm, 1), lambda i: (i, 0)),
            _resident((d, n)),
            _resident((1, DA_QK_W)),
            _resident((1, DA_QK_W)),
            _resident((DA_QK_W, DA_QK_W)),
            _resident((1, LANES)),
            _resident((1, LANES)),
        ],
        out_specs=[b for _, b in outs],
        compiler_params=_cparams(1),
        name="mix_proj",
    )(x, mod, g, pos, w, gq, gk, gsum, inv_a, inv_r)


SUBLANES = 8


REDUCE_SLAB_ROWS = 32


def _reduce_rows(x, reduce_fn):
    r, n = x.shape
    if r % REDUCE_SLAB_ROWS == 0 and r > REDUCE_SLAB_ROWS:
        x = reduce_fn(x.reshape(r // REDUCE_SLAB_ROWS, REDUCE_SLAB_ROWS, n), axis=0)
    return reduce_fn(x, axis=0, keepdims=True)


def _dattn_kernel(q_ref, k_ref, vt_ref, lq1_ref, lk1_ref, lq2_ref, lk2_ref, gs_ref, o_ref, *, lam_init, sub):
    k = k_ref[...]
    vt = vt_ref[...]
    lam = (jnp.exp(jnp.sum(lq1_ref[...] * lk1_ref[...], axis=-1, keepdims=True))
           - jnp.exp(jnp.sum(lq2_ref[...] * lk2_ref[...], axis=-1, keepdims=True)) + lam_init)
    lane = lax.broadcasted_iota(jnp.int32, (sub, LANES), 1)
    nsub = q_ref.shape[0] // sub

    def scores(t):
        q = q_ref[t * sub:(t + 1) * sub, :]
        zero = jnp.zeros_like(q)
        qq = jnp.concatenate([jnp.where(lane < DA_HEAD_DIM, q, zero), jnp.where(lane >= DA_HEAD_DIM, q, zero)],
                             axis=0)
        return _dot_nt(k, qq)

    def softmax(st):
        m = _reduce_rows(st, jnp.max)
        p = jnp.exp2(st - m)
        return p.astype(BF16), _reduce_rows(p, jnp.sum)

    def values(t, p, l):
        ot = jnp.dot(vt, p, preferred_element_type=F32)
        o = (ot[:, :sub] * (1.0 / l[:, :sub]) - ot[:, sub:] * (lam / l[:, sub:])).T
        ms = jnp.mean(o * o, axis=-1, keepdims=True)
        o_ref[t * sub:(t + 1) * sub, :] = (
            o * lax.rsqrt(ms + EPS) * gs_ref[...] * (1.0 - lam_init)).astype(o_ref.dtype)

    st = scores(0)
    pending = None
    for t in range(nsub):
        if pending is not None:
            values(t - 1, *pending)
        st_next = scores(t + 1) if t + 1 < nsub else None
        pending = softmax(st)
        st = st_next
    values(nsub - 1, *pending)


def _dattn(qa, ka, vat, lq1, lk1, lq2, lk2, gs, *, bsz, seq, lam_init, tq=2048, sub=256):
    t = qa.shape[0]
    tq = min(tq, seq)
    nq = seq // tq
    lam_spec = _resident((1, DA_HEAD_DIM))
    return pl.pallas_call(
        functools.partial(_dattn_kernel, lam_init=lam_init, sub=sub),
        out_shape=jax.ShapeDtypeStruct((t, DA_V_W), BF16),
        grid=(bsz, DA_HEADS, nq),
        in_specs=[
            pl.BlockSpec((tq, LANES), lambda b, h, i: (b * nq + i, h)),
            pl.BlockSpec((seq, LANES), lambda b, h, i: (b, h)),
            pl.BlockSpec((LANES, seq), lambda b, h, i: (h, b)),
            lam_spec, lam_spec, lam_spec, lam_spec,
            _resident((1, DA_V_DIM)),
        ],
        out_specs=pl.BlockSpec((tq, LANES), lambda b, h, i: (b * nq + i, h)),
        compiler_params=_cparams(3),
        name="diff_attn",
    )(qa, ka, vat, lq1, lk1, lq2, lk2, gs)


def _ret_kernel(q_ref, k_ref, kt_ref, v_ref, gate_ref, df_ref, db_ref, gn_ref, o_ref,
                dmask_ref, zf_ref, xf_ref, zb_ref, xb_ref, *, chunk):
    hh = pl.program_id(0)
    seq = q_ref.shape[0]
    nchunk = seq // chunk
    hl = lax.broadcasted_iota(jnp.int32, df_ref.shape, 1) == hh

    def head_log_gamma(ref):
        lg = jax.nn.log_sigmoid(ref[...])
        return jnp.sum(jnp.where(hl, lg, 0.0), axis=-1, keepdims=True)

    lgf = head_log_gamma(df_ref)
    lgb = head_log_gamma(db_ref)

    @pl.when(pl.program_id(1) == 0)
    def _():
        a = lax.broadcasted_iota(jnp.int32, (chunk, chunk), 0)
        b = lax.broadcasted_iota(jnp.int32, (chunk, chunk), 1)
        rel = (a - b).astype(F32)
        dec = jnp.exp(jnp.where(rel >= 0, lgf, -lgb) * rel)
        dmask_ref[...] = jnp.where(rel == 0, 2.0, dec)
        r = lax.broadcasted_iota(jnp.int32, (chunk, LANES), 0).astype(F32)
        zf_ref[...] = jnp.exp((chunk - 1.0 - r) * lgf)
        xf_ref[...] = jnp.exp((r + 1.0) * lgf)
        zb_ref[...] = jnp.exp(r * lgb)
        xb_ref[...] = jnp.exp((chunk - r) * lgb)

    def rows(n):
        return slice(n * chunk, (n + 1) * chunk)

    kvf, kvb = [], []
    for n in range(nchunk):
        vn = v_ref[rows(n), :].astype(F32)
        ktn = kt_ref[:, rows(n)]
        kvf.append(jnp.dot(ktn, (vn * zf_ref[...]).astype(BF16), preferred_element_type=F32))
        kvb.append(jnp.dot(ktn, (vn * zb_ref[...]).astype(BF16), preferred_element_type=F32))
    gcf = jnp.exp(chunk * lgf)
    gcb = jnp.exp(chunk * lgb)
    rf, rb = [None] * nchunk, [None] * nchunk
    r = jnp.zeros((LANES, RET_V_DIM), F32)
    for n in range(nchunk):
        rf[n] = r
        r = r * gcf + kvf[n]
    r = jnp.zeros((LANES, RET_V_DIM), F32)
    for n in reversed(range(nchunk)):
        rb[n] = r
        r = r * gcb + kvb[n]

    lane = lax.broadcasted_iota(jnp.int32, (chunk, LANES), 1)
    mine = (lane // RET_QK_DIM) == (hh % 2)
    for n in range(nchunk):
        qn = q_ref[rows(n), :]
        qn = jnp.where(mine, qn, jnp.zeros_like(qn))
        s = _dot_nt(qn, k_ref[rows(n), :])
        qf = qn.astype(F32)
        lhs = jnp.concatenate([(s * dmask_ref[...]).astype(BF16), (qf * xf_ref[...]).astype(BF16),
                               (qf * xb_ref[...]).astype(BF16)], axis=1)
        rhs = jnp.concatenate([v_ref[rows(n), :], rf[n].astype(BF16), rb[n].astype(BF16)], axis=0)
        y = jnp.dot(lhs, rhs, preferred_element_type=F32)
        ms = jnp.mean(y * y, axis=-1, keepdims=True)
        y = y * lax.rsqrt(ms + EPS) * gn_ref[...]
        o_ref[rows(n), :] = (y * _silu(gate_ref[rows(n), :].astype(F32))).astype(o_ref.dtype)


def _retention(qr, kr, krt, vr, gate, df, db, gn, *, bsz, seq, chunk=256):
    t = qr.shape[0]
    return pl.pallas_call(
        functools.partial(_ret_kernel, chunk=chunk),
        out_shape=jax.ShapeDtypeStruct((t, RET_V_W), BF16),
        grid=(RET_HEADS, bsz),
        in_specs=[
            pl.BlockSpec((seq, LANES), lambda h, b: (b, h // 2)),
            pl.BlockSpec((seq, LANES), lambda h, b: (b, h // 2)),
            pl.BlockSpec((LANES, seq), lambda h, b: (h // 2, b)),
            pl.BlockSpec((seq, LANES), lambda h, b: (b, h)),
            pl.BlockSpec((seq, LANES), lambda h, b: (b, h)),
            _resident((1, RET_HEADS)),
            _resident((1, RET_HEADS)),
            _resident((1, RET_V_DIM)),
        ],
        out_specs=pl.BlockSpec((seq, LANES), lambda h, b: (b, h)),
        scratch_shapes=[pltpu.VMEM((chunk, chunk), F32)] + [pltpu.VMEM((chunk, LANES), F32)] * 4,
        compiler_params=_cparams(2),
        name="retention",
    )(qr, kr, krt, vr, gate, df, db, gn)


def _merge_kernel(x_ref, mod_ref, g_ref, oa_ref, yr_ref, wm_ref, bm_ref, wa_ref, wr_ref, wo_ref, o_ref):
    x = x_ref[...]
    d = x.shape[1]
    sh, sc, gt = _mod_rows(mod_ref, 3)
    h = _modnorm(x, g_ref[...], sc, sh).astype(BF16)
    ml = jnp.dot(h, wm_ref[...], preferred_element_type=F32)
    pa = jnp.dot(oa_ref[...], wa_ref[...], preferred_element_type=F32)
    pr = jnp.dot(yr_ref[...], wr_ref[...], preferred_element_type=F32)
    merged = (jax.nn.sigmoid(ml[:, :d] + bm_ref[0:1, :]) * pa
              + jax.nn.sigmoid(ml[:, d:] + bm_ref[1:2, :]) * pr)
    out = jnp.dot(merged.astype(BF16), wo_ref[...], preferred_element_type=F32)
    o_ref[...] = x + gt * out


def _merge(x, mod, g, oa, yr, wm, bm, wa, wr, wo, *, seq, tm=512):
    t, d = x.shape
    return pl.pallas_call(
        _merge_kernel,
        out_shape=jax.ShapeDtypeStruct((t, d), F32),
        grid=(t // tm,),
        in_specs=[
            pl.BlockSpec((tm, d), lambda i: (i, 0)),
            pl.BlockSpec((1, N_MOD, d), lambda i: (i * tm // seq, 0, 0)),
            _resident((1, d)),
            pl.BlockSpec((tm, DA_V_W), lambda i: (i, 0)),
            pl.BlockSpec((tm, RET_V_W), lambda i: (i, 0)),
            _resident((d, 2 * d)),
            _resident((2, d)),
            _resident((DA_V_W, d)),
            _resident((RET_V_W, d)),
            _resident((d, d)),
        ],
        out_specs=pl.BlockSpec((tm, d), lambda i: (i, 0)),
        compiler_params=_cparams(1),
        name="mix_merge",
    )(x, mod, g, oa, yr, wm, bm, wa, wr, wo)


def _deinterleave_heads(w, heads, dim):
    d = w.shape[0]
    return w.reshape(d, heads, dim // 2, 2).transpose(0, 1, 3, 2).reshape(d, heads * dim)


def _lane_freqs(inv_half):
    return jnp.tile(inv_half, LANES // inv_half.shape[0]).reshape(1, LANES)


def kernel(x, c, positions, w_ada, b_ada, norm_ffn1, ffn1_w1, ffn1_w3, ffn1_w2, norm_mix, w_in, b_merge,
           da_q_gain, da_k_gain, da_lambda_q1, da_lambda_k1, da_lambda_q2, da_lambda_k2, da_subln,
           ret_decay_f, ret_decay_b, ret_norm, w_branch_a, w_branch_r, w_out, norm_ffn2, ffn2_w1, ffn2_w3,
           ffn2_w2):
    bsz, seq, d = x.shape
    depth = w_ada.shape[0]
    t = bsz * seq
    xt = x.reshape(t, d)
    pos = positions.reshape(t, 1)

    inv_a = _lane_freqs(1.0 / (ROPE_THETA ** (jnp.arange(0, DA_HEAD_DIM, 2, dtype=F32) / DA_HEAD_DIM)))
    inv_r = _lane_freqs(1.0 / (ROPE_THETA ** jnp.linspace(0.0, 1.0, RET_QK_DIM // 2, dtype=F32)))
    n_groups = DA_QK_W // DA_HEAD_DIM
    gsum = jnp.kron(jnp.eye(n_groups, dtype=F32), jnp.ones((DA_HEAD_DIM, DA_HEAD_DIM), F32)).astype(BF16)

    o_qr = 2 * DA_QK_W + DA_V_W
    o_kr = o_qr + RET_QK_W
    o_vr = o_kr + RET_QK_W
    o_merge = o_vr + 2 * RET_V_W

    for l in range(depth):
        lam_init = 0.8 - 0.6 * math.exp(-0.3 * l)
        mod = _ada(c, w_ada[l], b_ada[l].reshape(1, -1)).reshape(bsz, N_MOD, d)

        xt = _ffn(xt, mod, norm_ffn1[l].reshape(1, d), ffn1_w1[l].astype(BF16), ffn1_w3[l].astype(BF16),
                  ffn1_w2[l].astype(BF16), k0=0, seq=seq)

        wl = w_in[l]
        w_proj = jnp.concatenate([
            wl[:, :o_qr],
            _deinterleave_heads(wl[:, o_qr:o_kr], RET_HEADS, RET_QK_DIM),
            _deinterleave_heads(wl[:, o_kr:o_vr], RET_HEADS, RET_QK_DIM),
            wl[:, o_vr:o_merge]], axis=1).astype(BF16)
        qa, ka, vat, qr, kr, krt, vr, gate = _proj(
            xt, mod, norm_mix[l].reshape(1, d), pos, w_proj,
            jnp.tile(da_q_gain[l], n_groups).reshape(1, DA_QK_W),
            jnp.tile(da_k_gain[l], n_groups).reshape(1, DA_QK_W),
            gsum, inv_a, inv_r, seq=seq)

        oa = _dattn(qa, ka, vat, da_lambda_q1[l].reshape(1, -1), da_lambda_k1[l].reshape(1, -1),
                    da_lambda_q2[l].reshape(1, -1), da_lambda_k2[l].reshape(1, -1),
                    da_subln[l].reshape(1, -1), bsz=bsz, seq=seq, lam_init=lam_init)
        yr = _retention(qr, kr, krt, vr, gate, ret_decay_f[l].reshape(1, -1), ret_decay_b[l].reshape(1, -1),
                        ret_norm[l].reshape(1, -1), bsz=bsz, seq=seq)

        xt = _merge(xt, mod, norm_mix[l].reshape(1, d), oa, yr, wl[:, o_merge:].astype(BF16), b_merge[l],
                    w_branch_a[l].astype(BF16), w_branch_r[l].astype(BF16), w_out[l].astype(BF16), seq=seq)

        xt = _ffn(xt, mod, norm_ffn2[l].reshape(1, d), ffn2_w1[l].astype(BF16), ffn2_w3[l].astype(BF16),
                  ffn2_w2[l].astype(BF16), k0=6, seq=seq)
    return xt.reshape(bsz, seq, d)
```

```python
import functools
import math

import jax
import jax.numpy as jnp
from jax import lax
from jax.experimental import pallas as pl
from jax.experimental.pallas import tpu as pltpu

F32 = jnp.float32
BF16 = jnp.bfloat16

DA_HEADS = 4
DA_HEAD_DIM = 64
DA_V_DIM = 2 * DA_HEAD_DIM
RET_HEADS = 4
RET_QK_DIM = 64
RET_V_DIM = 128
ROPE_THETA = 10000.0
EPS = 1e-6
N_MOD = 9
LOG2E = math.log2(math.e)

DA_QK_W = DA_HEADS * 2 * DA_HEAD_DIM
DA_V_W = DA_HEADS * DA_V_DIM
RET_QK_W = RET_HEADS * RET_QK_DIM
RET_V_W = RET_HEADS * RET_V_DIM
PROJ_W = 2 * DA_QK_W + DA_V_W + 2 * RET_QK_W + 2 * RET_V_W

LANES = 128
BF16_SUBLANES = 16
VMEM_LIMIT_BYTES = 56 * 1024 * 1024


def _cparams(n_axes, flags=None):
    return pltpu.CompilerParams(
        flags=flags,
        dimension_semantics=("arbitrary",) * n_axes,
        vmem_limit_bytes=VMEM_LIMIT_BYTES,
    )


def _resident(shape):
    nd = len(shape)
    return pl.BlockSpec(shape, lambda *_: (0,) * nd, pipeline_mode=pl.Buffered(1))


def _silu(a):
    return a * jax.nn.sigmoid(a)


def _modnorm(x, g, sc, sh):
    ms = jnp.mean(x * x, axis=-1, keepdims=True)
    y = x * lax.rsqrt(ms + EPS) * g
    return y * (1.0 + sc) + sh


def _mod_rows(mod_ref, k0):
    return (mod_ref[0, k0:k0 + 1, :], mod_ref[0, k0 + 1:k0 + 2, :], mod_ref[0, k0 + 2:k0 + 3, :])


def _dot_nt(a, b):
    return lax.dot_general(a, b, (((1,), (1,)), ((), ())), preferred_element_type=F32)


def _ada_kernel(c_ref, w_ref, b_ref, o_ref):
    o_ref[...] = jnp.dot(_silu(c_ref[...]), w_ref[...], preferred_element_type=F32) + b_ref[...]


def _ada(c, w, b, tn=1024):
    bsz, d = c.shape
    n = w.shape[1]
    return pl.pallas_call(
        _ada_kernel,
        out_shape=jax.ShapeDtypeStruct((bsz, n), F32),
        grid=(n // tn,),
        in_specs=[
            pl.BlockSpec((bsz, d), lambda j: (0, 0)),
            pl.BlockSpec((d, tn), lambda j: (0, j)),
            pl.BlockSpec((1, tn), lambda j: (0, j)),
        ],
        out_specs=pl.BlockSpec((bsz, tn), lambda j: (0, j)),
        compiler_params=_cparams(1),
        name="ada_mod",
    )(c, w, b)


def _ffn_kernel(*refs, k0, nchunk, ncast):
    x_ref, mod_ref, g_ref, w1_ref, w3_ref, w2_ref = refs[:6]
    cast_in = refs[6:6 + ncast]
    o_ref = refs[6 + ncast]
    cast_out = refs[7 + ncast:]
    x = x_ref[...]
    sh, sc, gt = _mod_rows(mod_ref, k0)
    h = _modnorm(x, g_ref[...], sc, sh).astype(BF16)
    fc = w1_ref.shape[1] // nchunk
    acc = None
    for c in range(nchunk):
        a = jnp.dot(h, w1_ref[:, c * fc:(c + 1) * fc], preferred_element_type=F32)
        b = jnp.dot(h, w3_ref[:, c * fc:(c + 1) * fc], preferred_element_type=F32)
        u = (_silu(a) * b).astype(BF16)
        d = jnp.dot(u, w2_ref[c * fc:(c + 1) * fc, :], preferred_element_type=F32)
        acc = d if acc is None else acc + d
    o_ref[...] = x + (0.5 * gt) * acc
    for src, dst in zip(cast_in, cast_out):
        dst[...] = src[...].astype(dst.dtype)


def _cast_slab_spec(rows, cols, nsteps):
    hold = 1
    while (rows * hold) % (nsteps * BF16_SUBLANES):
        hold *= 2
    slab = rows * hold // nsteps
    return pl.BlockSpec((slab, cols), lambda i: (i // hold, 0))


def _ffn(x, mod, g, w1, w3, w2, *, k0, seq, casts=(), tm=512, nchunk=2):
    t, d = x.shape
    f = w1.shape[1]
    nsteps = t // tm
    cast_specs = [_cast_slab_spec(w.shape[0], w.shape[1], nsteps) for w in casts]
    outs = pl.pallas_call(
        functools.partial(_ffn_kernel, k0=k0, nchunk=nchunk, ncast=len(casts)),
        out_shape=[jax.ShapeDtypeStruct((t, d), F32)] + [jax.ShapeDtypeStruct(w.shape, BF16) for w in casts],
        grid=(nsteps,),
        in_specs=[
            pl.BlockSpec((tm, d), lambda i: (i, 0)),
            pl.BlockSpec((1, N_MOD, d), lambda i: (i * tm // seq, 0, 0)),
            _resident((1, d)),
            _resident((d, f)),
            _resident((d, f)),
            _resident((f, d)),
        ] + cast_specs,
        out_specs=[pl.BlockSpec((tm, d), lambda i: (i, 0))] + cast_specs,
        compiler_params=_cparams(1),
        name=f"ffn_{k0}",
    )(x, mod, g, w1, w3, w2, *casts)
    return outs[0], outs[1:]


def _rot_tables(pos, inv, dist):
    ang = pos * inv
    c = jnp.cos(ang)
    s = jnp.sin(ang)
    lane = lax.broadcasted_iota(jnp.int32, (1, LANES), 1)
    first = (lane & dist) == 0
    return c, jnp.where(first, -s, 0.0), jnp.where(first, 0.0, s), dist


def _rot128(y, tabs):
    c, sa, sb, dist = tabs
    return y * c + pltpu.roll(y, LANES - dist, 1) * sa + pltpu.roll(y, dist, 1) * sb


def _proj_kernel(x_ref, mod_ref, g_ref, pos_ref, w_ref, gq_ref, gk_ref, gsum_ref, inv_a_ref, inv_r_ref,
                 qa_ref, ka_ref, vat_ref, qr_ref, kr_ref, krt_ref, vr_ref, gate_ref):
    x = x_ref[...]
    sh, sc, _ = _mod_rows(mod_ref, 3)
    h = _modnorm(x, g_ref[...], sc, sh).astype(BF16)
    z = jnp.dot(h, w_ref[...], preferred_element_type=F32)
    pos = pos_ref[...].astype(F32)
    tab_a = _rot_tables(pos, inv_a_ref[...], DA_HEAD_DIM // 2)
    tab_r = _rot_tables(pos, inv_r_ref[...], 1)

    def qk_norm_rot(zs, gain, out_ref):
        ss = jnp.dot((zs * zs).astype(BF16), gsum_ref[...], preferred_element_type=F32)
        y = zs * lax.rsqrt(ss * (1.0 / DA_HEAD_DIM) + EPS) * gain
        for j in range(DA_QK_W // LANES):
            sl = slice(j * LANES, (j + 1) * LANES)
            out_ref[:, sl] = _rot128(y[:, sl], tab_a).astype(out_ref.dtype)

    o = 0
    qk_norm_rot(z[:, o:o + DA_QK_W], gq_ref[...] * (DA_HEAD_DIM ** -0.5 * LOG2E), qa_ref)
    o += DA_QK_W
    qk_norm_rot(z[:, o:o + DA_QK_W], gk_ref[...], ka_ref)
    o += DA_QK_W
    vat_ref[...] = z[:, o:o + DA_V_W].T.astype(vat_ref.dtype)
    o += DA_V_W
    for j in range(RET_QK_W // LANES):
        qr_ref[:, j * LANES:(j + 1) * LANES] = _rot128(
            z[:, o + j * LANES:o + (j + 1) * LANES], tab_r).astype(qr_ref.dtype)
    o += RET_QK_W
    for j in range(RET_QK_W // LANES):
        kj = _rot128(z[:, o + j * LANES:o + (j + 1) * LANES], tab_r) * (RET_QK_DIM ** -0.5)
        kr_ref[:, j * LANES:(j + 1) * LANES] = kj.astype(kr_ref.dtype)
        krt_ref[j * LANES:(j + 1) * LANES, :] = kj.T.astype(krt_ref.dtype)
    o += RET_QK_W
    vr_ref[...] = z[:, o:o + RET_V_W].astype(vr_ref.dtype)
    o += RET_V_W
    gate_ref[...] = z[:, o:o + RET_V_W].astype(gate_ref.dtype)


def _proj(x, mod, g, pos, w, gq, gk, gsum, inv_a, inv_r, *, seq, tm=512):
    t, d = x.shape
    n = PROJ_W

    def rows(width):
        return jax.ShapeDtypeStruct((t, width), BF16), pl.BlockSpec((tm, width), lambda i: (i, 0))

    def cols(width):
        return jax.ShapeDtypeStruct((width, t), BF16), pl.BlockSpec((width, tm), lambda i: (0, i))

    outs = [rows(DA_QK_W), rows(DA_QK_W), cols(DA_V_W), rows(RET_QK_W), rows(RET_QK_W), cols(RET_QK_W),
            rows(RET_V_W), rows(RET_V_W)]
    return pl.pallas_call(
        _proj_kernel,
        out_shape=[s for s, _ in outs],
        grid=(t // tm,),
        in_specs=[
            pl.BlockSpec((tm, d), lambda i: (i, 0)),
            pl.BlockSpec((1, N_MOD, d), lambda i: (i * tm // seq, 0, 0)),
            _resident((1, d)),
            pl.BlockSpec((tm, 1), lambda i: (i, 0)),
            _resident((d, n)),
            _resident((1, DA_QK_W)),
            _resident((1, DA_QK_W)),
            _resident((DA_QK_W, DA_QK_W)),
            _resident((1, LANES)),
            _resident((1, LANES)),
        ],
        out_specs=[b for _, b in outs],
        compiler_params=_cparams(1),
        name="mix_proj",
    )(x, mod, g, pos, w, gq, gk, gsum, inv_a, inv_r)


SUBLANES = 8


REDUCE_SLAB_ROWS = 8


def _reduce_rows(x, reduce_fn):
    r, n = x.shape
    if r % REDUCE_SLAB_ROWS == 0 and r > REDUCE_SLAB_ROWS:
        x = reduce_fn(x.reshape(r // REDUCE_SLAB_ROWS, REDUCE_SLAB_ROWS, n), axis=0)
    return reduce_fn(x, axis=0, keepdims=True)


def _slab_reduce(x, reduce_fn):
    r, n = x.shape
    return reduce_fn(x.reshape(r // REDUCE_SLAB_ROWS, REDUCE_SLAB_ROWS, n), axis=0)


def _dattn_kernel(q_ref, k_ref, vt_ref, lq1_ref, lk1_ref, lq2_ref, lk2_ref, gs_ref, o_ref, *,
                  lam_init, sub, kchunk):
    seq = k_ref.shape[0]
    nsub = q_ref.shape[0] // sub
    nch = seq // kchunk
    lam = (jnp.exp(jnp.sum(lq1_ref[...] * lk1_ref[...], axis=-1, keepdims=True))
           - jnp.exp(jnp.sum(lq2_ref[...] * lk2_ref[...], axis=-1, keepdims=True)) + lam_init)
    lane = lax.broadcasted_iota(jnp.int32, (sub, LANES), 1)

    def query_columns(t):
        q = q_ref[t * sub:(t + 1) * sub, :]
        zero = jnp.zeros_like(q)
        return jnp.concatenate([jnp.where(lane < DA_HEAD_DIM, q, zero),
                                jnp.where(lane >= DA_HEAD_DIM, q, zero)], axis=0)

    def finish(t, ot, l):
        o = (ot[:, :sub] * (1.0 / l[:, :sub]) - ot[:, sub:] * (lam / l[:, sub:])).T
        ms = jnp.mean(o * o, axis=-1, keepdims=True)
        o_ref[t * sub:(t + 1) * sub, :] = (
            o * lax.rsqrt(ms + EPS) * gs_ref[...] * (1.0 - lam_init)).astype(o_ref.dtype)

    st = m = p = l = None
    for it in range(nsub + 2):
        run_a, run_b, run_c = it < nsub, 1 <= it <= nsub, it >= 2
        qq = query_columns(it) if run_a else None
        st_new, p_new = [], []
        mx = ls = ot = None
        for c in range(nch):
            rows = slice(c * kchunk, (c + 1) * kchunk)
            if run_a:
                s_c = _dot_nt(k_ref[rows, :], qq)
                st_new.append(s_c)
                r = _slab_reduce(s_c, jnp.max)
                mx = r if mx is None else jnp.maximum(mx, r)
            if run_b:
                p_c = jnp.exp2(st[c] - m)
                r = _slab_reduce(p_c, jnp.sum)
                ls = r if ls is None else ls + r
                p_new.append(p_c.astype(BF16))
            if run_c:
                d = jnp.dot(vt_ref[:, rows], p[c], preferred_element_type=F32)
                ot = d if ot is None else ot + d
        if run_c:
            finish(it - 2, ot, l)
        if run_b:
            p, l = p_new, jnp.sum(ls, axis=0, keepdims=True)
        if run_a:
            st, m = st_new, jnp.max(mx, axis=0, keepdims=True)


def _dattn(qa, ka, vat, lq1, lk1, lq2, lk2, gs, *, bsz, seq, lam_init, tq=2048, sub=128, kchunk=512):
    t = qa.shape[0]
    tq = min(tq, seq)
    kchunk = min(kchunk, seq)
    nq = seq // tq
    lam_spec = _resident((1, DA_HEAD_DIM))
    return pl.pallas_call(
        functools.partial(_dattn_kernel, lam_init=lam_init, sub=sub, kchunk=kchunk),
        out_shape=jax.ShapeDtypeStruct((t, DA_V_W), BF16),
        grid=(bsz, DA_HEADS, nq),
        in_specs=[
            pl.BlockSpec((tq, LANES), lambda b, h, i: (b * nq + i, h)),
            pl.BlockSpec((seq, LANES), lambda b, h, i: (b, h)),
            pl.BlockSpec((LANES, seq), lambda b, h, i: (h, b)),
            lam_spec, lam_spec, lam_spec, lam_spec,
            _resident((1, DA_V_DIM)),
        ],
        out_specs=pl.BlockSpec((tq, LANES), lambda b, h, i: (b * nq + i, h)),
        compiler_params=_cparams(3),
        name="diff_attn",
    )(qa, ka, vat, lq1, lk1, lq2, lk2, gs)


def _ret_kernel(q_ref, k_ref, kt_ref, v_ref, gate_ref, df_ref, db_ref, gn_ref, o_ref,
                dmask_ref, zf_ref, xf_ref, zb_ref, xb_ref, *, chunk):
    hh = pl.program_id(0)
    seq = q_ref.shape[0]
    nchunk = seq // chunk
    hl = lax.broadcasted_iota(jnp.int32, df_ref.shape, 1) == hh

    def head_log_gamma(ref):
        lg = jax.nn.log_sigmoid(ref[...])
        return jnp.sum(jnp.where(hl, lg, 0.0), axis=-1, keepdims=True)

    lgf = head_log_gamma(df_ref)
    lgb = head_log_gamma(db_ref)

    @pl.when(pl.program_id(1) == 0)
    def _():
        a = lax.broadcasted_iota(jnp.int32, (chunk, chunk), 0)
        b = lax.broadcasted_iota(jnp.int32, (chunk, chunk), 1)
        rel = (a - b).astype(F32)
        dec = jnp.exp(jnp.where(rel >= 0, lgf, -lgb) * rel)
        dmask_ref[...] = jnp.where(rel == 0, 2.0, dec)
        r = lax.broadcasted_iota(jnp.int32, (chunk, LANES), 0).astype(F32)
        zf_ref[...] = jnp.exp((chunk - 1.0 - r) * lgf)
        xf_ref[...] = jnp.exp((r + 1.0) * lgf)
        zb_ref[...] = jnp.exp(r * lgb)
        xb_ref[...] = jnp.exp((chunk - r) * lgb)

    def rows(n):
        return slice(n * chunk, (n + 1) * chunk)

    kvf, kvb = [], []
    for n in range(nchunk):
        vn = v_ref[rows(n), :].astype(F32)
        ktn = kt_ref[:, rows(n)]
        kvf.append(jnp.dot(ktn, (vn * zf_ref[...]).astype(BF16), preferred_element_type=F32))
        kvb.append(jnp.dot(ktn, (vn * zb_ref[...]).astype(BF16), preferred_element_type=F32))
    gcf = jnp.exp(chunk * lgf)
    gcb = jnp.exp(chunk * lgb)
    rf, rb = [None] * nchunk, [None] * nchunk
    r = jnp.zeros((LANES, RET_V_DIM), F32)
    for n in range(nchunk):
        rf[n] = r
        r = r * gcf + kvf[n]
    r = jnp.zeros((LANES, RET_V_DIM), F32)
    for n in reversed(range(nchunk)):
        rb[n] = r
        r = r * gcb + kvb[n]

    lane = lax.broadcasted_iota(jnp.int32, (chunk, LANES), 1)
    mine = (lane // RET_QK_DIM) == (hh % 2)
    for n in range(nchunk):
        qn = q_ref[rows(n), :]
        qn = jnp.where(mine, qn, jnp.zeros_like(qn))
        s = _dot_nt(qn, k_ref[rows(n), :])
        qf = qn.astype(F32)
        lhs = jnp.concatenate([(s * dmask_ref[...]).astype(BF16), (qf * xf_ref[...]).astype(BF16),
                               (qf * xb_ref[...]).astype(BF16)], axis=1)
        rhs = jnp.concatenate([v_ref[rows(n), :], rf[n].astype(BF16), rb[n].astype(BF16)], axis=0)
        y = jnp.dot(lhs, rhs, preferred_element_type=F32)
        ms = jnp.mean(y * y, axis=-1, keepdims=True)
        y = y * lax.rsqrt(ms + EPS) * gn_ref[...]
        o_ref[rows(n), :] = (y * _silu(gate_ref[rows(n), :].astype(F32))).astype(o_ref.dtype)


def _retention(qr, kr, krt, vr, gate, df, db, gn, *, bsz, seq, chunk=256):
    t = qr.shape[0]
    return pl.pallas_call(
        functools.partial(_ret_kernel, chunk=chunk),
        out_shape=jax.ShapeDtypeStruct((t, RET_V_W), BF16),
        grid=(RET_HEADS, bsz),
        in_specs=[
            pl.BlockSpec((seq, LANES), lambda h, b: (b, h // 2)),
            pl.BlockSpec((seq, LANES), lambda h, b: (b, h // 2)),
            pl.BlockSpec((LANES, seq), lambda h, b: (h // 2, b)),
            pl.BlockSpec((seq, LANES), lambda h, b: (b, h)),
            pl.BlockSpec((seq, LANES), lambda h, b: (b, h)),
            _resident((1, RET_HEADS)),
            _resident((1, RET_HEADS)),
            _resident((1, RET_V_DIM)),
        ],
        out_specs=pl.BlockSpec((seq, LANES), lambda h, b: (b, h)),
        scratch_shapes=[pltpu.VMEM((chunk, chunk), F32)] + [pltpu.VMEM((chunk, LANES), F32)] * 4,
        compiler_params=_cparams(2),
        name="retention",
    )(qr, kr, krt, vr, gate, df, db, gn)


def _merge_kernel(x_ref, mod_ref, g_ref, oa_ref, yr_ref, wma_ref, wmr_ref, bm_ref, wa_ref, wr_ref, wo_ref, o_ref):
    x = x_ref[...]
    sh, sc, gt = _mod_rows(mod_ref, 3)
    h = _modnorm(x, g_ref[...], sc, sh).astype(BF16)
    ga = jnp.dot(h, wma_ref[...], preferred_element_type=F32)
    gr = jnp.dot(h, wmr_ref[...], preferred_element_type=F32)
    pa = jnp.dot(oa_ref[...], wa_ref[...], preferred_element_type=F32)
    pr = jnp.dot(yr_ref[...], wr_ref[...], preferred_element_type=F32)
    merged = jax.nn.sigmoid(ga + bm_ref[0:1, :]) * pa + jax.nn.sigmoid(gr + bm_ref[1:2, :]) * pr
    out = jnp.dot(merged.astype(BF16), wo_ref[...], preferred_element_type=F32)
    o_ref[...] = x + gt * out


def _merge(x, mod, g, oa, yr, w_in, bm, wa, wr, wo, *, seq, tm=512):
    t, d = x.shape
    gate_blk = PROJ_W // d

    def gate_spec(j):
        return pl.BlockSpec((d, d), lambda i: (0, gate_blk + j), pipeline_mode=pl.Buffered(1))

    return pl.pallas_call(
        _merge_kernel,
        out_shape=jax.ShapeDtypeStruct((t, d), F32),
        grid=(t // tm,),
        in_specs=[
            pl.BlockSpec((tm, d), lambda i: (i, 0)),
            pl.BlockSpec((1, N_MOD, d), lambda i: (i * tm // seq, 0, 0)),
            _resident((1, d)),
            pl.BlockSpec((tm, DA_V_W), lambda i: (i, 0)),
            pl.BlockSpec((tm, RET_V_W), lambda i: (i, 0)),
            gate_spec(0),
            gate_spec(1),
            _resident((2, d)),
            _resident((DA_V_W, d)),
            _resident((RET_V_W, d)),
            _resident((d, d)),
        ],
        out_specs=pl.BlockSpec((tm, d), lambda i: (i, 0)),
        compiler_params=_cparams(1),
        name="mix_merge",
    )(x, mod, g, oa, yr, w_in, w_in, bm, wa, wr, wo)


def _lane_freqs(inv_half, repeat):
    return jnp.tile(jnp.repeat(inv_half, repeat), LANES // (repeat * inv_half.shape[0])).reshape(1, LANES)


def kernel(x, c, positions, w_ada, b_ada, norm_ffn1, ffn1_w1, ffn1_w3, ffn1_w2, norm_mix, w_in, b_merge,
           da_q_gain, da_k_gain, da_lambda_q1, da_lambda_k1, da_lambda_q2, da_lambda_k2, da_subln,
           ret_decay_f, ret_decay_b, ret_norm, w_branch_a, w_branch_r, w_out, norm_ffn2, ffn2_w1, ffn2_w3,
           ffn2_w2):
    bsz, seq, d = x.shape
    depth = w_ada.shape[0]
    t = bsz * seq
    xt = x.reshape(t, d)
    pos = positions.reshape(t, 1)

    inv_a = _lane_freqs(1.0 / (ROPE_THETA ** (jnp.arange(0, DA_HEAD_DIM, 2, dtype=F32) / DA_HEAD_DIM)), 1)
    inv_r = _lane_freqs(1.0 / (ROPE_THETA ** jnp.linspace(0.0, 1.0, RET_QK_DIM // 2, dtype=F32)), 2)
    n_groups = DA_QK_W // DA_HEAD_DIM
    gsum = jnp.kron(jnp.eye(n_groups, dtype=F32), jnp.ones((DA_HEAD_DIM, DA_HEAD_DIM), F32)).astype(BF16)

    for l in range(depth):
        lam_init = 0.8 - 0.6 * math.exp(-0.3 * l)
        mod = _ada(c, w_ada[l], b_ada[l].reshape(1, -1)).reshape(bsz, N_MOD, d)

        xt, (w_in_b, wa_b, wr_b, wo_b, f2w1_b, f2w3_b, f2w2_b) = _ffn(
            xt, mod, norm_ffn1[l].reshape(1, d), ffn1_w1[l].astype(BF16), ffn1_w3[l].astype(BF16),
            ffn1_w2[l].astype(BF16), k0=0, seq=seq,
            casts=(w_in[l], w_branch_a[l], w_branch_r[l], w_out[l], ffn2_w1[l], ffn2_w3[l], ffn2_w2[l]))

        qa, ka, vat, qr, kr, krt, vr, gate = _proj(
            xt, mod, norm_mix[l].reshape(1, d), pos, w_in_b,
            jnp.tile(da_q_gain[l], n_groups).reshape(1, DA_QK_W),
            jnp.tile(da_k_gain[l], n_groups).reshape(1, DA_QK_W),
            gsum, inv_a, inv_r, seq=seq)

        oa = _dattn(qa, ka, vat, da_lambda_q1[l].reshape(1, -1), da_lambda_k1[l].reshape(1, -1),
                    da_lambda_q2[l].reshape(1, -1), da_lambda_k2[l].reshape(1, -1),
                    da_subln[l].reshape(1, -1), bsz=bsz, seq=seq, lam_init=lam_init)
        yr = _retention(qr, kr, krt, vr, gate, ret_decay_f[l].reshape(1, -1), ret_decay_b[l].reshape(1, -1),
                        ret_norm[l].reshape(1, -1), bsz=bsz, seq=seq)

        xt = _merge(xt, mod, norm_mix[l].reshape(1, d), oa, yr, w_in_b, b_merge[l], wa_b, wr_b, wo_b, seq=seq)

        xt, _ = _ffn(xt, mod, norm_ffn2[l].reshape(1, d), f2w1_b, f2w3_b, f2w2_b, k0=6, seq=seq)
    return xt.reshape(bsz, seq, d)
```

```python
import functools
import math

import jax
import jax.numpy as jnp
from jax import lax
from jax.experimental import pallas as pl
from jax.experimental.pallas import tpu as pltpu

F32 = jnp.float32
BF16 = jnp.bfloat16

DA_HEADS = 4
DA_HEAD_DIM = 64
DA_V_DIM = 2 * DA_HEAD_DIM
RET_HEADS = 4
RET_QK_DIM = 64
RET_V_DIM = 128
ROPE_THETA = 10000.0
EPS = 1e-6
N_MOD = 9
LOG2E = math.log2(math.e)

DA_QK_W = DA_HEADS * 2 * DA_HEAD_DIM
DA_V_W = DA_HEADS * DA_V_DIM
RET_QK_W = RET_HEADS * RET_QK_DIM
RET_V_W = RET_HEADS * RET_V_DIM
PROJ_W = 2 * DA_QK_W + DA_V_W + 2 * RET_QK_W + 2 * RET_V_W

LANES = 128
BF16_SUBLANES = 16
VMEM_LIMIT_BYTES = 56 * 1024 * 1024


def _cparams(n_axes, flags=None):
    return pltpu.CompilerParams(
        flags=flags,
        dimension_semantics=("arbitrary",) * n_axes,
        vmem_limit_bytes=VMEM_LIMIT_BYTES,
    )


def _resident(shape):
    nd = len(shape)
    return pl.BlockSpec(shape, lambda *_: (0,) * nd, pipeline_mode=pl.Buffered(1))


def _silu(a):
    return a * jax.nn.sigmoid(a)


def _modnorm(x, g, sc, sh):
    ms = jnp.mean(x * x, axis=-1, keepdims=True)
    y = x * lax.rsqrt(ms + EPS) * g
    return y * (1.0 + sc) + sh


def _mod_rows(mod_ref, k0):
    return (mod_ref[0, k0:k0 + 1, :], mod_ref[0, k0 + 1:k0 + 2, :], mod_ref[0, k0 + 2:k0 + 3, :])


def _dot_nt(a, b):
    return lax.dot_general(a, b, (((1,), (1,)), ((), ())), preferred_element_type=F32)


def _ada_kernel(c_ref, w_ref, b_ref, o_ref):
    o_ref[...] = jnp.dot(_silu(c_ref[...]), w_ref[...], preferred_element_type=F32) + b_ref[...]


def _ada(c, w, b, tn=1024):
    bsz, d = c.shape
    n = w.shape[1]
    return pl.pallas_call(
        _ada_kernel,
        out_shape=jax.ShapeDtypeStruct((bsz, n), F32),
        grid=(n // tn,),
        in_specs=[
            pl.BlockSpec((bsz, d), lambda j: (0, 0)),
            pl.BlockSpec((d, tn), lambda j: (0, j)),
            pl.BlockSpec((1, tn), lambda j: (0, j)),
        ],
        out_specs=pl.BlockSpec((bsz, tn), lambda j: (0, j)),
        compiler_params=_cparams(1),
        name="ada_mod",
    )(c, w, b)


def _ffn_kernel(*refs, k0, sub, ncast):
    x_ref, mod_ref, g_ref, w1_ref, w3_ref, w2_ref = refs[:6]
    cast_in = refs[6:6 + ncast]
    o_ref = refs[6 + ncast]
    cast_out = refs[7 + ncast:]
    sh, sc, gt = _mod_rows(mod_ref, k0)
    nsub = x_ref.shape[0] // sub

    def hidden(s):
        return _modnorm(x_ref[s * sub:(s + 1) * sub, :], g_ref[...], sc, sh).astype(BF16)

    h = hidden(0)
    for s in range(nsub):
        rows = slice(s * sub, (s + 1) * sub)
        a = jnp.dot(h, w1_ref[...], preferred_element_type=F32)
        h_next = hidden(s + 1) if s + 1 < nsub else None
        b = jnp.dot(h, w3_ref[...], preferred_element_type=F32)
        u = (_silu(a) * b).astype(BF16)
        d = jnp.dot(u, w2_ref[...], preferred_element_type=F32)
        o_ref[rows, :] = x_ref[rows, :] + (0.5 * gt) * d
        h = h_next
    for src, dst in zip(cast_in, cast_out):
        dst[...] = src[...].astype(dst.dtype)


def _cast_slab_spec(rows, cols, nsteps):
    hold = 1
    while (rows * hold) % (nsteps * BF16_SUBLANES):
        hold *= 2
    slab = rows * hold // nsteps
    return pl.BlockSpec((slab, cols), lambda i: (i // hold, 0))


def _ffn(x, mod, g, w1, w3, w2, *, k0, seq, casts=(), tm=1024, sub=256):
    t, d = x.shape
    f = w1.shape[1]
    tm = min(tm, seq)
    nsteps = t // tm
    cast_specs = [_cast_slab_spec(w.shape[0], w.shape[1], nsteps) for w in casts]
    outs = pl.pallas_call(
        functools.partial(_ffn_kernel, k0=k0, sub=sub, ncast=len(casts)),
        out_shape=[jax.ShapeDtypeStruct((t, d), F32)] + [jax.ShapeDtypeStruct(w.shape, BF16) for w in casts],
        grid=(nsteps,),
        in_specs=[
            pl.BlockSpec((tm, d), lambda i: (i, 0)),
            pl.BlockSpec((1, N_MOD, d), lambda i: (i * tm // seq, 0, 0)),
            _resident((1, d)),
            _resident((d, f)),
            _resident((d, f)),
            _resident((f, d)),
        ] + cast_specs,
        out_specs=[pl.BlockSpec((tm, d), lambda i: (i, 0))] + cast_specs,
        compiler_params=_cparams(1),
        name=f"ffn_{k0}",
    )(x, mod, g, w1, w3, w2, *casts)
    return outs[0], outs[1:]


def _rot_tables(pos, inv, dist):
    ang = pos * inv
    c = jnp.cos(ang)
    s = jnp.sin(ang)
    lane = lax.broadcasted_iota(jnp.int32, (1, LANES), 1)
    first = (lane & dist) == 0
    return c, jnp.where(first, -s, 0.0), jnp.where(first, 0.0, s), dist


def _rot128(y, tabs):
    c, sa, sb, dist = tabs
    return y * c + pltpu.roll(y, LANES - dist, 1) * sa + pltpu.roll(y, dist, 1) * sb


def _proj_kernel(x_ref, mod_ref, g_ref, pos_ref, w_ref, gq_ref, gk_ref, gsum_ref, inv_a_ref, inv_r_ref,
                 qa_ref, ka_ref, vat_ref, qr_ref, kr_ref, krt_ref, vr_ref, gate_ref):
    x = x_ref[...]
    sh, sc, _ = _mod_rows(mod_ref, 3)
    h = _modnorm(x, g_ref[...], sc, sh).astype(BF16)
    z = jnp.dot(h, w_ref[...], preferred_element_type=F32)
    pos = pos_ref[...].astype(F32)
    tab_a = _rot_tables(pos, inv_a_ref[...], DA_HEAD_DIM // 2)
    tab_r = _rot_tables(pos, inv_r_ref[...], 1)

    def qk_norm_rot(zs, gain, out_ref):
        ss = jnp.dot((zs * zs).astype(BF16), gsum_ref[...], preferred_element_type=F32)
        y = zs * lax.rsqrt(ss * (1.0 / DA_HEAD_DIM) + EPS) * gain
        for j in range(DA_QK_W // LANES):
            sl = slice(j * LANES, (j + 1) * LANES)
            out_ref[:, sl] = _rot128(y[:, sl], tab_a).astype(out_ref.dtype)

    o = 0
    qk_norm_rot(z[:, o:o + DA_QK_W], gq_ref[...] * (DA_HEAD_DIM ** -0.5 * LOG2E), qa_ref)
    o += DA_QK_W
    qk_norm_rot(z[:, o:o + DA_QK_W], gk_ref[...], ka_ref)
    o += DA_QK_W
    vat_ref[...] = z[:, o:o + DA_V_W].T.astype(vat_ref.dtype)
    o += DA_V_W
    for j in range(RET_QK_W // LANES):
        qr_ref[:, j * LANES:(j + 1) * LANES] = _rot128(
            z[:, o + j * LANES:o + (j + 1) * LANES], tab_r).astype(qr_ref.dtype)
    o += RET_QK_W
    for j in range(RET_QK_W // LANES):
        kj = _rot128(z[:, o + j * LANES:o + (j + 1) * LANES], tab_r) * (RET_QK_DIM ** -0.5)
        kr_ref[:, j * LANES:(j + 1) * LANES] = kj.astype(kr_ref.dtype)
        krt_ref[j * LANES:(j + 1) * LANES, :] = kj.T.astype(krt_ref.dtype)
    o += RET_QK_W
    vr_ref[...] = z[:, o:o + RET_V_W].astype(vr_ref.dtype)
    o += RET_V_W
    gate_ref[...] = z[:, o:o + RET_V_W].astype(gate_ref.dtype)


def _proj(x, mod, g, pos, w, gq, gk, gsum, inv_a, inv_r, *, seq, tm=512):
    t, d = x.shape
    n = PROJ_W

    def rows(width):
        return jax.ShapeDtypeStruct((t, width), BF16), pl.BlockSpec((tm, width), lambda i: (i, 0))

    def cols(width):
        return jax.ShapeDtypeStruct((width, t), BF16), pl.BlockSpec((width, tm), lambda i: (0, i))

    outs = [rows(DA_QK_W), rows(DA_QK_W), cols(DA_V_W), rows(RET_QK_W), rows(RET_QK_W), cols(RET_QK_W),
            rows(RET_V_W), rows(RET_V_W)]
    return pl.pallas_call(
        _proj_kernel,
        out_shape=[s for s, _ in outs],
        grid=(t // tm,),
        in_specs=[
            pl.BlockSpec((tm, d), lambda i: (i, 0)),
            pl.BlockSpec((1, N_MOD, d), lambda i: (i * tm // seq, 0, 0)),
            _resident((1, d)),
            pl.BlockSpec((tm, 1), lambda i: (i, 0)),
            _resident((d, n)),
            _resident((1, DA_QK_W)),
            _resident((1, DA_QK_W)),
            _resident((DA_QK_W, DA_QK_W)),
            _resident((1, LANES)),
            _resident((1, LANES)),
        ],
        out_specs=[b for _, b in outs],
        compiler_params=_cparams(1),
        name="mix_proj",
    )(x, mod, g, pos, w, gq, gk, gsum, inv_a, inv_r)


SUBLANES = 8


REDUCE_SLAB_ROWS = 8


def _reduce_rows(x, reduce_fn):
    r, n = x.shape
    if r % REDUCE_SLAB_ROWS == 0 and r > REDUCE_SLAB_ROWS:
        x = reduce_fn(x.reshape(r // REDUCE_SLAB_ROWS, REDUCE_SLAB_ROWS, n), axis=0)
    return reduce_fn(x, axis=0, keepdims=True)


def _slab_reduce(x, reduce_fn):
    r, n = x.shape
    return reduce_fn(x.reshape(r // REDUCE_SLAB_ROWS, REDUCE_SLAB_ROWS, n), axis=0)


def _dattn_kernel(q_ref, k_ref, vt_ref, lq1_ref, lk1_ref, lq2_ref, lk2_ref, gs_ref, o_ref, *,
                  lam_init, sub, kchunk):
    seq = k_ref.shape[0]
    nsub = q_ref.shape[0] // sub
    nch = seq // kchunk
    lam = (jnp.exp(jnp.sum(lq1_ref[...] * lk1_ref[...], axis=-1, keepdims=True))
           - jnp.exp(jnp.sum(lq2_ref[...] * lk2_ref[...], axis=-1, keepdims=True)) + lam_init)
    lane = lax.broadcasted_iota(jnp.int32, (sub, LANES), 1)

    def query_columns(t):
        q = q_ref[t * sub:(t + 1) * sub, :]
        zero = jnp.zeros_like(q)
        return jnp.concatenate([jnp.where(lane < DA_HEAD_DIM, q, zero),
                                jnp.where(lane >= DA_HEAD_DIM, q, zero)], axis=0)

    def finish(t, ot, l):
        o = (ot[:, :sub] * (1.0 / l[:, :sub]) - ot[:, sub:] * (lam / l[:, sub:])).T
        ms = jnp.mean(o * o, axis=-1, keepdims=True)
        o_ref[t * sub:(t + 1) * sub, :] = (
            o * lax.rsqrt(ms + EPS) * gs_ref[...] * (1.0 - lam_init)).astype(o_ref.dtype)

    st = m = p = l = None
    for it in range(nsub + 2):
        run_a, run_b, run_c = it < nsub, 1 <= it <= nsub, it >= 2
        qq = query_columns(it) if run_a else None
        st_new, p_new = [], []
        mx = ls = ot = None
        for c in range(nch):
            rows = slice(c * kchunk, (c + 1) * kchunk)
            if run_a:
                s_c = _dot_nt(k_ref[rows, :], qq)
                st_new.append(s_c)
                r = _slab_reduce(s_c, jnp.max)
                mx = r if mx is None else jnp.maximum(mx, r)
            if run_b:
                p_c = jnp.exp2(st[c] - m)
                r = _slab_reduce(p_c, jnp.sum)
                ls = r if ls is None else ls + r
                p_new.append(p_c.astype(BF16))
            if run_c:
                d = jnp.dot(vt_ref[:, rows], p[c], preferred_element_type=F32)
                ot = d if ot is None else ot + d
        if run_c:
            finish(it - 2, ot, l)
        if run_b:
            p, l = p_new, jnp.sum(ls, axis=0, keepdims=True)
        if run_a:
            st, m = st_new, jnp.max(mx, axis=0, keepdims=True)


def _dattn(qa, ka, vat, lq1, lk1, lq2, lk2, gs, *, bsz, seq, lam_init, tq=2048, sub=128, kchunk=512):
    t = qa.shape[0]
    tq = min(tq, seq)
    kchunk = min(kchunk, seq)
    nq = seq // tq
    lam_spec = _resident((1, DA_HEAD_DIM))
    return pl.pallas_call(
        functools.partial(_dattn_kernel, lam_init=lam_init, sub=sub, kchunk=kchunk),
        out_shape=jax.ShapeDtypeStruct((t, DA_V_W), BF16),
        grid=(bsz, DA_HEADS, nq),
        in_specs=[
            pl.BlockSpec((tq, LANES), lambda b, h, i: (b * nq + i, h)),
            pl.BlockSpec((seq, LANES), lambda b, h, i: (b, h)),
            pl.BlockSpec((LANES, seq), lambda b, h, i: (h, b)),
            lam_spec, lam_spec, lam_spec, lam_spec,
            _resident((1, DA_V_DIM)),
        ],
        out_specs=pl.BlockSpec((tq, LANES), lambda b, h, i: (b * nq + i, h)),
        compiler_params=_cparams(3),
        name="diff_attn",
    )(qa, ka, vat, lq1, lk1, lq2, lk2, gs)


def _ret_kernel(q_ref, k_ref, kt_ref, v_ref, gate_ref, df_ref, db_ref, gn_ref, o_ref,
                dmask_ref, zf_ref, xf_ref, zb_ref, xb_ref, *, chunk):
    hh = pl.program_id(0)
    seq = q_ref.shape[0]
    nchunk = seq // chunk
    hl = lax.broadcasted_iota(jnp.int32, df_ref.shape, 1) == hh

    def head_log_gamma(ref):
        lg = jax.nn.log_sigmoid(ref[...])
        return jnp.sum(jnp.where(hl, lg, 0.0), axis=-1, keepdims=True)

    lgf = head_log_gamma(df_ref)
    lgb = head_log_gamma(db_ref)

    @pl.when(pl.program_id(1) == 0)
    def _():
        a = lax.broadcasted_iota(jnp.int32, (chunk, chunk), 0)
        b = lax.broadcasted_iota(jnp.int32, (chunk, chunk), 1)
        rel = (a - b).astype(F32)
        dec = jnp.exp(jnp.where(rel >= 0, lgf, -lgb) * rel)
        dmask_ref[...] = jnp.where(rel == 0, 2.0, dec)
        r = lax.broadcasted_iota(jnp.int32, (chunk, LANES), 0).astype(F32)
        zf_ref[...] = jnp.exp((chunk - 1.0 - r) * lgf)
        xf_ref[...] = jnp.exp((r + 1.0) * lgf)
        zb_ref[...] = jnp.exp(r * lgb)
        xb_ref[...] = jnp.exp((chunk - r) * lgb)

    def rows(n):
        return slice(n * chunk, (n + 1) * chunk)

    kvf, kvb = [], []
    for n in range(nchunk):
        vn = v_ref[rows(n), :].astype(F32)
        ktn = kt_ref[:, rows(n)]
        kvf.append(jnp.dot(ktn, (vn * zf_ref[...]).astype(BF16), preferred_element_type=F32))
        kvb.append(jnp.dot(ktn, (vn * zb_ref[...]).astype(BF16), preferred_element_type=F32))
    gcf = jnp.exp(chunk * lgf)
    gcb = jnp.exp(chunk * lgb)
    rf, rb = [None] * nchunk, [None] * nchunk
    r = jnp.zeros((LANES, RET_V_DIM), F32)
    for n in range(nchunk):
        rf[n] = r
        r = r * gcf + kvf[n]
    r = jnp.zeros((LANES, RET_V_DIM), F32)
    for n in reversed(range(nchunk)):
        rb[n] = r
        r = r * gcb + kvb[n]

    lane = lax.broadcasted_iota(jnp.int32, (chunk, LANES), 1)
    mine = (lane // RET_QK_DIM) == (hh % 2)
    for n in range(nchunk):
        qn = q_ref[rows(n), :]
        qn = jnp.where(mine, qn, jnp.zeros_like(qn))
        s = _dot_nt(qn, k_ref[rows(n), :])
        qf = qn.astype(F32)
        lhs = jnp.concatenate([(s * dmask_ref[...]).astype(BF16), (qf * xf_ref[...]).astype(BF16),
                               (qf * xb_ref[...]).astype(BF16)], axis=1)
        rhs = jnp.concatenate([v_ref[rows(n), :], rf[n].astype(BF16), rb[n].astype(BF16)], axis=0)
        y = jnp.dot(lhs, rhs, preferred_element_type=F32)
        ms = jnp.mean(y * y, axis=-1, keepdims=True)
        y = y * lax.rsqrt(ms + EPS) * gn_ref[...]
        o_ref[rows(n), :] = (y * _silu(gate_ref[rows(n), :].astype(F32))).astype(o_ref.dtype)


def _retention(qr, kr, krt, vr, gate, df, db, gn, *, bsz, seq, chunk=256):
    t = qr.shape[0]
    return pl.pallas_call(
        functools.partial(_ret_kernel, chunk=chunk),
        out_shape=jax.ShapeDtypeStruct((t, RET_V_W), BF16),
        grid=(RET_HEADS, bsz),
        in_specs=[
            pl.BlockSpec((seq, LANES), lambda h, b: (b, h // 2)),
            pl.BlockSpec((seq, LANES), lambda h, b: (b, h // 2)),
            pl.BlockSpec((LANES, seq), lambda h, b: (h // 2, b)),
            pl.BlockSpec((seq, LANES), lambda h, b: (b, h)),
            pl.BlockSpec((seq, LANES), lambda h, b: (b, h)),
            _resident((1, RET_HEADS)),
            _resident((1, RET_HEADS)),
            _resident((1, RET_V_DIM)),
        ],
        out_specs=pl.BlockSpec((seq, LANES), lambda h, b: (b, h)),
        scratch_shapes=[pltpu.VMEM((chunk, chunk), F32)] + [pltpu.VMEM((chunk, LANES), F32)] * 4,
        compiler_params=_cparams(2),
        name="retention",
    )(qr, kr, krt, vr, gate, df, db, gn)


def _merge_kernel(x_ref, mod_ref, g_ref, oa_ref, yr_ref, wma_ref, wmr_ref, bm_ref, wa_ref, wr_ref, wo_ref, o_ref):
    x = x_ref[...]
    sh, sc, gt = _mod_rows(mod_ref, 3)
    h = _modnorm(x, g_ref[...], sc, sh).astype(BF16)
    ga = jnp.dot(h, wma_ref[...], preferred_element_type=F32)
    gr = jnp.dot(h, wmr_ref[...], preferred_element_type=F32)
    pa = jnp.dot(oa_ref[...], wa_ref[...], preferred_element_type=F32)
    pr = jnp.dot(yr_ref[...], wr_ref[...], preferred_element_type=F32)
    merged = jax.nn.sigmoid(ga + bm_ref[0:1, :]) * pa + jax.nn.sigmoid(gr + bm_ref[1:2, :]) * pr
    out = jnp.dot(merged.astype(BF16), wo_ref[...], preferred_element_type=F32)
    o_ref[...] = x + gt * out


def _merge(x, mod, g, oa, yr, w_in, bm, wa, wr, wo, *, seq, tm=512):
    t, d = x.shape
    gate_blk = PROJ_W // d

    def gate_spec(j):
        return pl.BlockSpec((d, d), lambda i: (0, gate_blk + j), pipeline_mode=pl.Buffered(1))

    return pl.pallas_call(
        _merge_kernel,
        out_shape=jax.ShapeDtypeStruct((t, d), F32),
        grid=(t // tm,),
        in_specs=[
            pl.BlockSpec((tm, d), lambda i: (i, 0)),
            pl.BlockSpec((1, N_MOD, d), lambda i: (i * tm // seq, 0, 0)),
            _resident((1, d)),
            pl.BlockSpec((tm, DA_V_W), lambda i: (i, 0)),
            pl.BlockSpec((tm, RET_V_W), lambda i: (i, 0)),
            gate_spec(0),
            gate_spec(1),
            _resident((2, d)),
            _resident((DA_V_W, d)),
            _resident((RET_V_W, d)),
            _resident((d, d)),
        ],
        out_specs=pl.BlockSpec((tm, d), lambda i: (i, 0)),
        compiler_params=_cparams(1),
        name="mix_merge",
    )(x, mod, g, oa, yr, w_in, w_in, bm, wa, wr, wo)


def _lane_freqs(inv_half, repeat):
    return jnp.tile(jnp.repeat(inv_half, repeat), LANES // (repeat * inv_half.shape[0])).reshape(1, LANES)


def kernel(x, c, positions, w_ada, b_ada, norm_ffn1, ffn1_w1, ffn1_w3, ffn1_w2, norm_mix, w_in, b_merge,
           da_q_gain, da_k_gain, da_lambda_q1, da_lambda_k1, da_lambda_q2, da_lambda_k2, da_subln,
           ret_decay_f, ret_decay_b, ret_norm, w_branch_a, w_branch_r, w_out, norm_ffn2, ffn2_w1, ffn2_w3,
           ffn2_w2):
    bsz, seq, d = x.shape
    depth = w_ada.shape[0]
    t = bsz * seq
    xt = x.reshape(t, d)
    pos = positions.reshape(t, 1)

    inv_a = _lane_freqs(1.0 / (ROPE_THETA ** (jnp.arange(0, DA_HEAD_DIM, 2, dtype=F32) / DA_HEAD_DIM)), 1)
    inv_r = _lane_freqs(1.0 / (ROPE_THETA ** jnp.linspace(0.0, 1.0, RET_QK_DIM // 2, dtype=F32)), 2)
    n_groups = DA_QK_W // DA_HEAD_DIM
    gsum = jnp.kron(jnp.eye(n_groups, dtype=F32), jnp.ones((DA_HEAD_DIM, DA_HEAD_DIM), F32)).astype(BF16)

    for l in range(depth):
        lam_init = 0.8 - 0.6 * math.exp(-0.3 * l)
        mod = _ada(c, w_ada[l], b_ada[l].reshape(1, -1)).reshape(bsz, N_MOD, d)

        xt, (w_in_b, wa_b, wr_b, wo_b, f2w1_b, f2w3_b, f2w2_b) = _ffn(
            xt, mod, norm_ffn1[l].reshape(1, d), ffn1_w1[l].astype(BF16), ffn1_w3[l].astype(BF16),
            ffn1_w2[l].astype(BF16), k0=0, seq=seq,
            casts=(w_in[l], w_branch_a[l], w_branch_r[l], w_out[l], ffn2_w1[l], ffn2_w3[l], ffn2_w2[l]))

        qa, ka, vat, qr, kr, krt, vr, gate = _proj(
            xt, mod, norm_mix[l].reshape(1, d), pos, w_in_b,
            jnp.tile(da_q_gain[l], n_groups).reshape(1, DA_QK_W),
            jnp.tile(da_k_gain[l], n_groups).reshape(1, DA_QK_W),
            gsum, inv_a, inv_r, seq=seq)

        oa = _dattn(qa, ka, vat, da_lambda_q1[l].reshape(1, -1), da_lambda_k1[l].reshape(1, -1),
                    da_lambda_q2[l].reshape(1, -1), da_lambda_k2[l].reshape(1, -1),
                    da_subln[l].reshape(1, -1), bsz=bsz, seq=seq, lam_init=lam_init)
        yr = _retention(qr, kr, krt, vr, gate, ret_decay_f[l].reshape(1, -1), ret_decay_b[l].reshape(1, -1),
                        ret_norm[l].reshape(1, -1), bsz=bsz, seq=seq)

        xt = _merge(xt, mod, norm_mix[l].reshape(1, d), oa, yr, w_in_b, b_merge[l], wa_b, wr_b, wo_b, seq=seq)

        xt, _ = _ffn(xt, mod, norm_ffn2[l].reshape(1, d), f2w1_b, f2w3_b, f2w2_b, k0=6, seq=seq)
    return xt.reshape(bsz, seq, d)
```

```python
import functools
import math

import jax
import jax.numpy as jnp
from jax import lax
from jax.experimental import pallas as pl
from jax.experimental.pallas import tpu as pltpu

F32 = jnp.float32
BF16 = jnp.bfloat16

DA_HEADS = 4
DA_HEAD_DIM = 64
DA_V_DIM = 2 * DA_HEAD_DIM
RET_HEADS = 4
RET_QK_DIM = 64
RET_V_DIM = 128
ROPE_THETA = 10000.0
EPS = 1e-6
N_MOD = 9
LOG2E = math.log2(math.e)

DA_QK_W = DA_HEADS * 2 * DA_HEAD_DIM
DA_V_W = DA_HEADS * DA_V_DIM
RET_QK_W = RET_HEADS * RET_QK_DIM
RET_V_W = RET_HEADS * RET_V_DIM
PROJ_W = 2 * DA_QK_W + DA_V_W + 2 * RET_QK_W + 2 * RET_V_W

LANES = 128
BF16_SUBLANES = 16
VMEM_LIMIT_BYTES = 56 * 1024 * 1024


def _cparams(n_axes, flags=None):
    return pltpu.CompilerParams(
        flags=flags,
        dimension_semantics=("arbitrary",) * n_axes,
        vmem_limit_bytes=VMEM_LIMIT_BYTES,
    )


def _resident(shape):
    nd = len(shape)
    return pl.BlockSpec(shape, lambda *_: (0,) * nd, pipeline_mode=pl.Buffered(1))


def _silu(a):
    return a * jax.nn.sigmoid(a)


def _modnorm(x, g, sc, sh):
    ms = jnp.mean(x * x, axis=-1, keepdims=True)
    y = x * lax.rsqrt(ms + EPS) * g
    return y * (1.0 + sc) + sh


def _mod_rows(mod_ref, k0):
    return (mod_ref[0, k0:k0 + 1, :], mod_ref[0, k0 + 1:k0 + 2, :], mod_ref[0, k0 + 2:k0 + 3, :])


def _dot_nt(a, b):
    return lax.dot_general(a, b, (((1,), (1,)), ((), ())), preferred_element_type=F32)


def _token_cols(ref, rows):
    tile = ref.shape[2]
    start, size = rows.start, rows.stop - rows.start
    assert start // tile == (start + size - 1) // tile
    return ref[start // tile, :, start % tile:start % tile + size]


def _ada_kernel(c_ref, w_ref, b_ref, o_ref):
    o_ref[...] = jnp.dot(_silu(c_ref[...]), w_ref[...], preferred_element_type=F32) + b_ref[...]


def _ada(c, w, b, tn=1024):
    bsz, d = c.shape
    n = w.shape[1]
    return pl.pallas_call(
        _ada_kernel,
        out_shape=jax.ShapeDtypeStruct((bsz, n), F32),
        grid=(n // tn,),
        in_specs=[
            pl.BlockSpec((bsz, d), lambda j: (0, 0)),
            pl.BlockSpec((d, tn), lambda j: (0, j)),
            pl.BlockSpec((1, tn), lambda j: (0, j)),
        ],
        out_specs=pl.BlockSpec((bsz, tn), lambda j: (0, j)),
        compiler_params=_cparams(1),
        name="ada_mod",
    )(c, w, b)


def _ffn_kernel(*refs, k0, sub, ncast):
    x_ref, mod_ref, g_ref, w1_ref, w3_ref, w2_ref = refs[:6]
    cast_in = refs[6:6 + ncast]
    o_ref = refs[6 + ncast]
    cast_out = refs[7 + ncast:]
    sh, sc, gt = _mod_rows(mod_ref, k0)
    nsub = x_ref.shape[0] // sub

    def hidden(s):
        return _modnorm(x_ref[s * sub:(s + 1) * sub, :], g_ref[...], sc, sh).astype(BF16)

    h = hidden(0)
    for s in range(nsub):
        rows = slice(s * sub, (s + 1) * sub)
        a = jnp.dot(h, w1_ref[...], preferred_element_type=F32)
        h_next = hidden(s + 1) if s + 1 < nsub else None
        b = jnp.dot(h, w3_ref[...], preferred_element_type=F32)
        u = (_silu(a) * b).astype(BF16)
        d = jnp.dot(u, w2_ref[...], preferred_element_type=F32)
        o_ref[rows, :] = x_ref[rows, :] + (0.5 * gt) * d
        h = h_next
    for src, dst in zip(cast_in, cast_out):
        dst[...] = src[...].astype(dst.dtype)


def _cast_slab_spec(rows, cols, nsteps):
    hold = 1
    while (rows * hold) % (nsteps * BF16_SUBLANES):
        hold *= 2
    slab = rows * hold // nsteps
    return pl.BlockSpec((slab, cols), lambda i: (i // hold, 0))


def _ffn(x, mod, g, w1, w3, w2, *, k0, seq, casts=(), tm=1024, sub=256):
    t, d = x.shape
    f = w1.shape[1]
    tm = min(tm, seq)
    nsteps = t // tm
    cast_specs = [_cast_slab_spec(w.shape[0], w.shape[1], nsteps) for w in casts]
    outs = pl.pallas_call(
        functools.partial(_ffn_kernel, k0=k0, sub=sub, ncast=len(casts)),
        out_shape=[jax.ShapeDtypeStruct((t, d), F32)] + [jax.ShapeDtypeStruct(w.shape, BF16) for w in casts],
        grid=(nsteps,),
        in_specs=[
            pl.BlockSpec((tm, d), lambda i: (i, 0)),
            pl.BlockSpec((1, N_MOD, d), lambda i: (i * tm // seq, 0, 0)),
            _resident((1, d)),
            _resident((d, f)),
            _resident((d, f)),
            _resident((f, d)),
        ] + cast_specs,
        out_specs=[pl.BlockSpec((tm, d), lambda i: (i, 0))] + cast_specs,
        compiler_params=_cparams(1),
        name=f"ffn_{k0}",
    )(x, mod, g, w1, w3, w2, *casts)
    return outs[0], outs[1:]


def _rot_tables(pos, inv, dist):
    ang = pos * inv
    c = jnp.cos(ang)
    s = jnp.sin(ang)
    lane = lax.broadcasted_iota(jnp.int32, (1, LANES), 1)
    first = (lane & dist) == 0
    return c, jnp.where(first, -s, 0.0), jnp.where(first, 0.0, s), dist


def _rot128(y, tabs):
    c, sa, sb, dist = tabs
    return y * c + pltpu.roll(y, LANES - dist, 1) * sa + pltpu.roll(y, dist, 1) * sb


def _proj_kernel(x_ref, mod_ref, g_ref, pos_ref, w_ref, gq_ref, gk_ref, gsum_ref, inv_a_ref, inv_r_ref,
                 qa_ref, ka_ref, vat_ref, qr_ref, kr_ref, krt_ref, vr_ref, gate_ref, *, sub):
    sh, sc, _ = _mod_rows(mod_ref, 3)
    nsub = x_ref.shape[0] // sub

    def hidden(s):
        return _modnorm(x_ref[s * sub:(s + 1) * sub, :], g_ref[...], sc, sh).astype(BF16)

    def qk_norm_rot(zs, gain, tab, out_ref, rows):
        ss = jnp.dot((zs * zs).astype(BF16), gsum_ref[...], preferred_element_type=F32)
        y = zs * lax.rsqrt(ss * (1.0 / DA_HEAD_DIM) + EPS) * gain
        for j in range(DA_QK_W // LANES):
            sl = slice(j * LANES, (j + 1) * LANES)
            out_ref[rows, sl] = _rot128(y[:, sl], tab).astype(out_ref.dtype)

    h = hidden(0)
    for s in range(nsub):
        rows = slice(s * sub, (s + 1) * sub)
        z = jnp.dot(h, w_ref[...], preferred_element_type=F32)
        h = hidden(s + 1) if s + 1 < nsub else None
        pos = pos_ref[rows, :].astype(F32)
        tab_a = _rot_tables(pos, inv_a_ref[...], DA_HEAD_DIM // 2)
        tab_r = _rot_tables(pos, inv_r_ref[...], 1)
        o = 0
        qk_norm_rot(z[:, o:o + DA_QK_W], gq_ref[...] * (DA_HEAD_DIM ** -0.5 * LOG2E), tab_a, qa_ref, rows)
        o += DA_QK_W
        qk_norm_rot(z[:, o:o + DA_QK_W], gk_ref[...], tab_a, ka_ref, rows)
        o += DA_QK_W
        vat_ref[0, :, rows] = z[:, o:o + DA_V_W].T.astype(vat_ref.dtype)
        o += DA_V_W
        for j in range(RET_QK_W // LANES):
            qr_ref[rows, j * LANES:(j + 1) * LANES] = _rot128(
                z[:, o + j * LANES:o + (j + 1) * LANES], tab_r).astype(qr_ref.dtype)
        o += RET_QK_W
        for j in range(RET_QK_W // LANES):
            kj = _rot128(z[:, o + j * LANES:o + (j + 1) * LANES], tab_r) * (RET_QK_DIM ** -0.5)
            kr_ref[rows, j * LANES:(j + 1) * LANES] = kj.astype(kr_ref.dtype)
            krt_ref[0, j * LANES:(j + 1) * LANES, rows] = kj.T.astype(krt_ref.dtype)
        o += RET_QK_W
        vr_ref[rows, :] = z[:, o:o + RET_V_W].astype(vr_ref.dtype)
        o += RET_V_W
        gate_ref[rows, :] = z[:, o:o + RET_V_W].astype(gate_ref.dtype)


def _proj(x, mod, g, pos, w, gq, gk, gsum, inv_a, inv_r, *, seq, tm=1024, sub=256):
    t, d = x.shape
    n = PROJ_W
    tm = min(tm, seq)

    def rows(width):
        return jax.ShapeDtypeStruct((t, width), BF16), pl.BlockSpec((tm, width), lambda i: (i, 0))

    def cols(width):
        return (jax.ShapeDtypeStruct((t // tm, width, tm), BF16),
                pl.BlockSpec((1, width, tm), lambda i: (i, 0, 0)))

    outs = [rows(DA_QK_W), rows(DA_QK_W), cols(DA_V_W), rows(RET_QK_W), rows(RET_QK_W), cols(RET_QK_W),
            rows(RET_V_W), rows(RET_V_W)]
    return pl.pallas_call(
        functools.partial(_proj_kernel, sub=sub),
        out_shape=[s for s, _ in outs],
        grid=(t // tm,),
        in_specs=[
            pl.BlockSpec((tm, d), lambda i: (i, 0)),
            pl.BlockSpec((1, N_MOD, d), lambda i: (i * tm // seq, 0, 0)),
            _resident((1, d)),
            pl.BlockSpec((tm, 1), lambda i: (i, 0)),
            _resident((d, n)),
            _resident((1, DA_QK_W)),
            _resident((1, DA_QK_W)),
            _resident((DA_QK_W, DA_QK_W)),
            _resident((1, LANES)),
            _resident((1, LANES)),
        ],
        out_specs=[b for _, b in outs],
        compiler_params=_cparams(1),
        name="mix_proj",
    )(x, mod, g, pos, w, gq, gk, gsum, inv_a, inv_r)


SUBLANES = 8


REDUCE_SLAB_ROWS = 8


def _reduce_rows(x, reduce_fn):
    r, n = x.shape
    if r % REDUCE_SLAB_ROWS == 0 and r > REDUCE_SLAB_ROWS:
        x = reduce_fn(x.reshape(r // REDUCE_SLAB_ROWS, REDUCE_SLAB_ROWS, n), axis=0)
    return reduce_fn(x, axis=0, keepdims=True)


def _slab_reduce(x, reduce_fn):
    r, n = x.shape
    return reduce_fn(x.reshape(r // REDUCE_SLAB_ROWS, REDUCE_SLAB_ROWS, n), axis=0)


def _dattn_kernel(q_ref, k_ref, vt_ref, lq1_ref, lk1_ref, lq2_ref, lk2_ref, gs_ref, o_ref, *,
                  lam_init, sub, kchunk):
    seq = k_ref.shape[0]
    nsub = q_ref.shape[0] // sub
    nch = seq // kchunk
    lam = (jnp.exp(jnp.sum(lq1_ref[...] * lk1_ref[...], axis=-1, keepdims=True))
           - jnp.exp(jnp.sum(lq2_ref[...] * lk2_ref[...], axis=-1, keepdims=True)) + lam_init)
    lane = lax.broadcasted_iota(jnp.int32, (sub, LANES), 1)

    def query_columns(t):
        q = q_ref[t * sub:(t + 1) * sub, :]
        zero = jnp.zeros_like(q)
        return jnp.concatenate([jnp.where(lane < DA_HEAD_DIM, q, zero),
                                jnp.where(lane >= DA_HEAD_DIM, q, zero)], axis=0)

    def finish(t, ot, l):
        o = (ot[:, :sub] * (1.0 / l[:, :sub]) - ot[:, sub:] * (lam / l[:, sub:])).T
        ms = jnp.mean(o * o, axis=-1, keepdims=True)
        o_ref[t * sub:(t + 1) * sub, :] = (
            o * lax.rsqrt(ms + EPS) * gs_ref[...] * (1.0 - lam_init)).astype(o_ref.dtype)

    st = m = p = l = None
    for it in range(nsub + 2):
        run_a, run_b, run_c = it < nsub, 1 <= it <= nsub, it >= 2
        qq = query_columns(it) if run_a else None
        st_new, p_new = [], []
        mx = ls = ot = None
        for c in range(nch):
            rows = slice(c * kchunk, (c + 1) * kchunk)
            if run_a:
                s_c = _dot_nt(k_ref[rows, :], qq)
                st_new.append(s_c)
                r = _slab_reduce(s_c, jnp.max)
                mx = r if mx is None else jnp.maximum(mx, r)
            if run_b:
                p_c = jnp.exp2(st[c] - m)
                r = _slab_reduce(p_c, jnp.sum)
                ls = r if ls is None else ls + r
                p_new.append(p_c.astype(BF16))
            if run_c:
                d = jnp.dot(_token_cols(vt_ref, rows), p[c], preferred_element_type=F32)
                ot = d if ot is None else ot + d
        if run_c:
            finish(it - 2, ot, l)
        if run_b:
            p, l = p_new, jnp.sum(ls, axis=0, keepdims=True)
        if run_a:
            st, m = st_new, jnp.max(mx, axis=0, keepdims=True)


def _dattn(qa, ka, vat, lq1, lk1, lq2, lk2, gs, *, bsz, seq, lam_init, tq=2048, sub=128, kchunk=512):
    t = qa.shape[0]
    tq = min(tq, seq)
    kchunk = min(kchunk, seq)
    nq = seq // tq
    lam_spec = _resident((1, DA_HEAD_DIM))
    return pl.pallas_call(
        functools.partial(_dattn_kernel, lam_init=lam_init, sub=sub, kchunk=kchunk),
        out_shape=jax.ShapeDtypeStruct((t, DA_V_W), BF16),
        grid=(bsz, DA_HEADS, nq),
        in_specs=[
            pl.BlockSpec((tq, LANES), lambda b, h, i: (b * nq + i, h)),
            pl.BlockSpec((seq, LANES), lambda b, h, i: (b, h)),
            pl.BlockSpec((seq // vat.shape[2], LANES, vat.shape[2]), lambda b, h, i: (b, h, 0)),
            lam_spec, lam_spec, lam_spec, lam_spec,
            _resident((1, DA_V_DIM)),
        ],
        out_specs=pl.BlockSpec((tq, LANES), lambda b, h, i: (b * nq + i, h)),
        compiler_params=_cparams(3),
        name="diff_attn",
    )(qa, ka, vat, lq1, lk1, lq2, lk2, gs)


def _ret_kernel(q_ref, k_ref, kt_ref, v_ref, gate_ref, df_ref, db_ref, gn_ref, o_ref,
                dmask_ref, zf_ref, xf_ref, zb_ref, xb_ref, *, chunk):
    hh = pl.program_id(0)
    seq = q_ref.shape[0]
    nchunk = seq // chunk
    hl = lax.broadcasted_iota(jnp.int32, df_ref.shape, 1) == hh

    def head_log_gamma(ref):
        lg = jax.nn.log_sigmoid(ref[...])
        return jnp.sum(jnp.where(hl, lg, 0.0), axis=-1, keepdims=True)

    lgf = head_log_gamma(df_ref)
    lgb = head_log_gamma(db_ref)

    @pl.when(pl.program_id(1) == 0)
    def _():
        a = lax.broadcasted_iota(jnp.int32, (chunk, chunk), 0)
        b = lax.broadcasted_iota(jnp.int32, (chunk, chunk), 1)
        rel = (a - b).astype(F32)
        dec = jnp.exp(jnp.where(rel >= 0, lgf, -lgb) * rel)
        dmask_ref[...] = jnp.where(rel == 0, 2.0, dec)
        r = lax.broadcasted_iota(jnp.int32, (chunk, LANES), 0).astype(F32)
        zf_ref[...] = jnp.exp((chunk - 1.0 - r) * lgf)
        xf_ref[...] = jnp.exp((r + 1.0) * lgf)
        zb_ref[...] = jnp.exp(r * lgb)
        xb_ref[...] = jnp.exp((chunk - r) * lgb)

    def rows(n):
        return slice(n * chunk, (n + 1) * chunk)

    kvf, kvb = [], []
    for n in range(nchunk):
        vn = v_ref[rows(n), :].astype(F32)
        ktn = _token_cols(kt_ref, rows(n))
        kvf.append(jnp.dot(ktn, (vn * zf_ref[...]).astype(BF16), preferred_element_type=F32))
        kvb.append(jnp.dot(ktn, (vn * zb_ref[...]).astype(BF16), preferred_element_type=F32))
    gcf = jnp.exp(chunk * lgf)
    gcb = jnp.exp(chunk * lgb)
    rf, rb = [None] * nchunk, [None] * nchunk
    r = jnp.zeros((LANES, RET_V_DIM), F32)
    for n in range(nchunk):
        rf[n] = r
        r = r * gcf + kvf[n]
    r = jnp.zeros((LANES, RET_V_DIM), F32)
    for n in reversed(range(nchunk)):
        rb[n] = r
        r = r * gcb + kvb[n]

    lane = lax.broadcasted_iota(jnp.int32, (chunk, LANES), 1)
    mine = (lane // RET_QK_DIM) == (hh % 2)
    for n in range(nchunk):
        qn = q_ref[rows(n), :]
        qn = jnp.where(mine, qn, jnp.zeros_like(qn))
        s = _dot_nt(qn, k_ref[rows(n), :])
        qf = qn.astype(F32)
        lhs = jnp.concatenate([(s * dmask_ref[...]).astype(BF16), (qf * xf_ref[...]).astype(BF16),
                               (qf * xb_ref[...]).astype(BF16)], axis=1)
        rhs = jnp.concatenate([v_ref[rows(n), :], rf[n].astype(BF16), rb[n].astype(BF16)], axis=0)
        y = jnp.dot(lhs, rhs, preferred_element_type=F32)
        ms = jnp.mean(y * y, axis=-1, keepdims=True)
        y = y * lax.rsqrt(ms + EPS) * gn_ref[...]
        o_ref[rows(n), :] = (y * _silu(gate_ref[rows(n), :].astype(F32))).astype(o_ref.dtype)


def _retention(qr, kr, krt, vr, gate, df, db, gn, *, bsz, seq, chunk=256):
    t = qr.shape[0]
    return pl.pallas_call(
        functools.partial(_ret_kernel, chunk=chunk),
        out_shape=jax.ShapeDtypeStruct((t, RET_V_W), BF16),
        grid=(RET_HEADS, bsz),
        in_specs=[
            pl.BlockSpec((seq, LANES), lambda h, b: (b, h // 2)),
            pl.BlockSpec((seq, LANES), lambda h, b: (b, h // 2)),
            pl.BlockSpec((seq // krt.shape[2], LANES, krt.shape[2]), lambda h, b: (b, h // 2, 0)),
            pl.BlockSpec((seq, LANES), lambda h, b: (b, h)),
            pl.BlockSpec((seq, LANES), lambda h, b: (b, h)),
            _resident((1, RET_HEADS)),
            _resident((1, RET_HEADS)),
            _resident((1, RET_V_DIM)),
        ],
        out_specs=pl.BlockSpec((seq, LANES), lambda h, b: (b, h)),
        scratch_shapes=[pltpu.VMEM((chunk, chunk), F32)] + [pltpu.VMEM((chunk, LANES), F32)] * 4,
        compiler_params=_cparams(2),
        name="retention",
    )(qr, kr, krt, vr, gate, df, db, gn)


def _merge_kernel(x_ref, mod_ref, g_ref, oa_ref, yr_ref, wma_ref, wmr_ref, bm_ref, wa_ref, wr_ref, wo_ref, o_ref,
                  *, sub):
    sh, sc, gt = _mod_rows(mod_ref, 3)
    nsub = x_ref.shape[0] // sub

    def hidden(s):
        return _modnorm(x_ref[s * sub:(s + 1) * sub, :], g_ref[...], sc, sh).astype(BF16)

    h = hidden(0)
    for s in range(nsub):
        rows = slice(s * sub, (s + 1) * sub)
        ga = jnp.dot(h, wma_ref[...], preferred_element_type=F32)
        gr = jnp.dot(h, wmr_ref[...], preferred_element_type=F32)
        h = hidden(s + 1) if s + 1 < nsub else None
        pa = jnp.dot(oa_ref[rows, :], wa_ref[...], preferred_element_type=F32)
        pr = jnp.dot(yr_ref[rows, :], wr_ref[...], preferred_element_type=F32)
        merged = jax.nn.sigmoid(ga + bm_ref[0:1, :]) * pa + jax.nn.sigmoid(gr + bm_ref[1:2, :]) * pr
        out = jnp.dot(merged.astype(BF16), wo_ref[...], preferred_element_type=F32)
        o_ref[rows, :] = x_ref[rows, :] + gt * out


def _merge(x, mod, g, oa, yr, w_in, bm, wa, wr, wo, *, seq, tm=1024, sub=256):
    t, d = x.shape
    tm = min(tm, seq)
    gate_blk = PROJ_W // d

    def gate_spec(j):
        return pl.BlockSpec((d, d), lambda i: (0, gate_blk + j), pipeline_mode=pl.Buffered(1))

    return pl.pallas_call(
        functools.partial(_merge_kernel, sub=sub),
        out_shape=jax.ShapeDtypeStruct((t, d), F32),
        grid=(t // tm,),
        in_specs=[
            pl.BlockSpec((tm, d), lambda i: (i, 0)),
            pl.BlockSpec((1, N_MOD, d), lambda i: (i * tm // seq, 0, 0)),
            _resident((1, d)),
            pl.BlockSpec((tm, DA_V_W), lambda i: (i, 0)),
            pl.BlockSpec((tm, RET_V_W), lambda i: (i, 0)),
            gate_spec(0),
            gate_spec(1),
            _resident((2, d)),
            _resident((DA_V_W, d)),
            _resident((RET_V_W, d)),
            _resident((d, d)),
        ],
        out_specs=pl.BlockSpec((tm, d), lambda i: (i, 0)),
        compiler_params=_cparams(1),
        name="mix_merge",
    )(x, mod, g, oa, yr, w_in, w_in, bm, wa, wr, wo)


def _lane_freqs(inv_half, repeat):
    return jnp.tile(jnp.repeat(inv_half, repeat), LANES // (repeat * inv_half.shape[0])).reshape(1, LANES)


def kernel(x, c, positions, w_ada, b_ada, norm_ffn1, ffn1_w1, ffn1_w3, ffn1_w2, norm_mix, w_in, b_merge,
           da_q_gain, da_k_gain, da_lambda_q1, da_lambda_k1, da_lambda_q2, da_lambda_k2, da_subln,
           ret_decay_f, ret_decay_b, ret_norm, w_branch_a, w_branch_r, w_out, norm_ffn2, ffn2_w1, ffn2_w3,
           ffn2_w2):
    bsz, seq, d = x.shape
    depth = w_ada.shape[0]
    t = bsz * seq
    xt = x.reshape(t, d)
    pos = positions.reshape(t, 1)

    inv_a = _lane_freqs(1.0 / (ROPE_THETA ** (jnp.arange(0, DA_HEAD_DIM, 2, dtype=F32) / DA_HEAD_DIM)), 1)
    inv_r = _lane_freqs(1.0 / (ROPE_THETA ** jnp.linspace(0.0, 1.0, RET_QK_DIM // 2, dtype=F32)), 2)
    n_groups = DA_QK_W // DA_HEAD_DIM
    gsum = jnp.kron(jnp.eye(n_groups, dtype=F32), jnp.ones((DA_HEAD_DIM, DA_HEAD_DIM), F32)).astype(BF16)

    for l in range(depth):
        lam_init = 0.8 - 0.6 * math.exp(-0.3 * l)
        mod = _ada(c, w_ada[l], b_ada[l].reshape(1, -1)).reshape(bsz, N_MOD, d)

        xt, (w_in_b, wa_b, wr_b, wo_b, f2w1_b, f2w3_b, f2w2_b) = _ffn(
            xt, mod, norm_ffn1[l].reshape(1, d), ffn1_w1[l].astype(BF16), ffn1_w3[l].astype(BF16),
            ffn1_w2[l].astype(BF16), k0=0, seq=seq,
            casts=(w_in[l], w_branch_a[l], w_branch_r[l], w_out[l], ffn2_w1[l], ffn2_w3[l], ffn2_w2[l]))

        qa, ka, vat, qr, kr, krt, vr, gate = _proj(
            xt, mod, norm_mix[l].reshape(1, d), pos, w_in_b,
            jnp.tile(da_q_gain[l], n_groups).reshape(1, DA_QK_W),
            jnp.tile(da_k_gain[l], n_groups).reshape(1, DA_QK_W),
            gsum, inv_a, inv_r, seq=seq)

        oa = _dattn(qa, ka, vat, da_lambda_q1[l].reshape(1, -1), da_lambda_k1[l].reshape(1, -1),
                    da_lambda_q2[l].reshape(1, -1), da_lambda_k2[l].reshape(1, -1),
                    da_subln[l].reshape(1, -1), bsz=bsz, seq=seq, lam_init=lam_init)
        yr = _retention(qr, kr, krt, vr, gate, ret_decay_f[l].reshape(1, -1), ret_decay_b[l].reshape(1, -1),
                        ret_norm[l].reshape(1, -1), bsz=bsz, seq=seq)

        xt = _merge(xt, mod, norm_mix[l].reshape(1, d), oa, yr, w_in_b, b_merge[l], wa_b, wr_b, wo_b, seq=seq)

        xt, _ = _ffn(xt, mod, norm_ffn2[l].reshape(1, d), f2w1_b, f2w3_b, f2w2_b, k0=6, seq=seq)
    return xt.reshape(bsz, seq, d)
```

```python
import functools
import math

import jax
import jax.numpy as jnp
from jax import lax
from jax.experimental import pallas as pl
from jax.experimental.pallas import tpu as pltpu

F32 = jnp.float32
BF16 = jnp.bfloat16

DA_HEADS = 4
DA_HEAD_DIM = 64
DA_V_DIM = 2 * DA_HEAD_DIM
RET_HEADS = 4
RET_QK_DIM = 64
RET_V_DIM = 128
ROPE_THETA = 10000.0
EPS = 1e-6
N_MOD = 9
LOG2E = math.log2(math.e)

DA_QK_W = DA_HEADS * 2 * DA_HEAD_DIM
DA_V_W = DA_HEADS * DA_V_DIM
RET_QK_W = RET_HEADS * RET_QK_DIM
RET_V_W = RET_HEADS * RET_V_DIM
PROJ_W = 2 * DA_QK_W + DA_V_W + 2 * RET_QK_W + 2 * RET_V_W

LANES = 128
BF16_SUBLANES = 16
VMEM_LIMIT_BYTES = 56 * 1024 * 1024


def _cparams(n_axes, flags=None):
    return pltpu.CompilerParams(
        flags=flags,
        dimension_semantics=("arbitrary",) * n_axes,
        vmem_limit_bytes=VMEM_LIMIT_BYTES,
    )


def _resident(shape):
    nd = len(shape)
    return pl.BlockSpec(shape, lambda *_: (0,) * nd, pipeline_mode=pl.Buffered(1))


def _silu(a):
    return a * jax.nn.sigmoid(a)


def _zero_after(v):
    return ((pltpu.bitcast(v, jnp.uint32) >> 16) >> 16).astype(F32)


def _modnorm(x, g, sc, sh, anchor=None):
    ms = jnp.mean(x * x, axis=-1, keepdims=True)
    if anchor is not None:
        ms = ms + anchor
    y = x * lax.rsqrt(ms + EPS) * g
    return y * (1.0 + sc) + sh


def _mod_rows(mod_ref, k0):
    return (mod_ref[0, k0:k0 + 1, :], mod_ref[0, k0 + 1:k0 + 2, :], mod_ref[0, k0 + 2:k0 + 3, :])


def _dot_nt(a, b):
    return lax.dot_general(a, b, (((1,), (1,)), ((), ())), preferred_element_type=F32)


def _token_cols(ref, rows):
    tile = ref.shape[2]
    start, size = rows.start, rows.stop - rows.start
    assert start // tile == (start + size - 1) // tile
    return ref[start // tile, :, start % tile:start % tile + size]


def _ada_kernel(c_ref, w_ref, b_ref, o_ref):
    o_ref[...] = jnp.dot(_silu(c_ref[...]), w_ref[...], preferred_element_type=F32) + b_ref[...]


def _ada(c, w, b, tn=1024):
    bsz, d = c.shape
    n = w.shape[1]
    return pl.pallas_call(
        _ada_kernel,
        out_shape=jax.ShapeDtypeStruct((bsz, n), F32),
        grid=(n // tn,),
        in_specs=[
            pl.BlockSpec((bsz, d), lambda j: (0, 0)),
            pl.BlockSpec((d, tn), lambda j: (0, j)),
            pl.BlockSpec((1, tn), lambda j: (0, j)),
        ],
        out_specs=pl.BlockSpec((bsz, tn), lambda j: (0, j)),
        compiler_params=_cparams(1),
        name="ada_mod",
    )(c, w, b)


def _ffn_kernel(*refs, k0, sub, ncast):
    x_ref, mod_ref, g_ref, w1_ref, w3_ref, w2_ref = refs[:6]
    cast_in = refs[6:6 + ncast]
    o_ref = refs[6 + ncast]
    cast_out = refs[7 + ncast:]
    sh, sc, gt = _mod_rows(mod_ref, k0)
    nsub = x_ref.shape[0] // sub

    def hidden(s):
        return _modnorm(x_ref[s * sub:(s + 1) * sub, :], g_ref[...], sc, sh).astype(BF16)

    h = hidden(0)
    for s in range(nsub):
        rows = slice(s * sub, (s + 1) * sub)
        a = jnp.dot(h, w1_ref[...], preferred_element_type=F32)
        h_next = hidden(s + 1) if s + 1 < nsub else None
        b = jnp.dot(h, w3_ref[...], preferred_element_type=F32)
        u = (_silu(a) * b).astype(BF16)
        d = jnp.dot(u, w2_ref[...], preferred_element_type=F32)
        o_ref[rows, :] = x_ref[rows, :] + (0.5 * gt) * d
        h = h_next
    for src, dst in zip(cast_in, cast_out):
        dst[...] = src[...].astype(dst.dtype)


def _cast_slab_spec(rows, cols, nsteps):
    hold = 1
    while (rows * hold) % (nsteps * BF16_SUBLANES):
        hold *= 2
    slab = rows * hold // nsteps
    return pl.BlockSpec((slab, cols), lambda i: (i // hold, 0))


def _ffn(x, mod, g, w1, w3, w2, *, k0, seq, casts=(), tm=1024, sub=256):
    t, d = x.shape
    f = w1.shape[1]
    tm = min(tm, seq)
    nsteps = t // tm
    cast_specs = [_cast_slab_spec(w.shape[0], w.shape[1], nsteps) for w in casts]
    outs = pl.pallas_call(
        functools.partial(_ffn_kernel, k0=k0, sub=sub, ncast=len(casts)),
        out_shape=[jax.ShapeDtypeStruct((t, d), F32)] + [jax.ShapeDtypeStruct(w.shape, BF16) for w in casts],
        grid=(nsteps,),
        in_specs=[
            pl.BlockSpec((tm, d), lambda i: (i, 0)),
            pl.BlockSpec((1, N_MOD, d), lambda i: (i * tm // seq, 0, 0)),
            _resident((1, d)),
            _resident((d, f)),
            _resident((d, f)),
            _resident((f, d)),
        ] + cast_specs,
        out_specs=[pl.BlockSpec((tm, d), lambda i: (i, 0))] + cast_specs,
        compiler_params=_cparams(1),
        name=f"ffn_{k0}",
    )(x, mod, g, w1, w3, w2, *casts)
    return outs[0], outs[1:]


def _rot_tables(c, s, dists):
    c_swapped = pltpu.roll(c, LANES // 2, 1)
    s_swapped = pltpu.roll(s, LANES // 2, 1)
    lane = lax.broadcasted_iota(jnp.int32, (1, LANES), 1)
    low = lane < LANES // 2
    tabs = []
    for dist, (ck, sk) in zip(dists, ((jnp.where(low, c, c_swapped), jnp.where(low, s, s_swapped)),
                                      (jnp.where(low, c_swapped, c), jnp.where(low, s_swapped, s)))):
        first = (lane & dist) == 0
        tabs.append((ck, jnp.where(first, -sk, 0.0), jnp.where(first, 0.0, sk), dist))
    return tabs


def _rot128(y, tabs):
    c, sa, sb, dist = tabs
    return y * c + pltpu.roll(y, LANES - dist, 1) * sa + pltpu.roll(y, dist, 1) * sb


def _proj_kernel(x_ref, mod_ref, g_ref, pos_ref, inv_ref, bcos_ref, bsin_ref, w_ref, gq_ref, gk_ref, gsum_ref,
                 qa_ref, ka_ref, vat_ref, qr_ref, kr_ref, krt_ref, vr_ref, gate_ref, cos_ref, sin_ref, *, sub):
    sh, sc, _ = _mod_rows(mod_ref, 3)
    nsub = x_ref.shape[0] // sub

    pos = pos_ref[...]
    inv = inv_ref[...]
    ang0 = pos[0:1, :].astype(F32) * inv
    c0, s0 = jnp.cos(ang0), jnp.sin(ang0)
    cos_ref[...] = c0 * bcos_ref[...] - s0 * bsin_ref[...]
    sin_ref[...] = s0 * bcos_ref[...] + c0 * bsin_ref[...]
    step = pos - pos[0:1, :] - lax.broadcasted_iota(jnp.int32, pos.shape, 0)

    @pl.when(jnp.max(jnp.where(step != 0, 1.0, 0.0)) > 0.0)
    def _():
        ang = pos.astype(F32) * inv
        cos_ref[...] = jnp.cos(ang)
        sin_ref[...] = jnp.sin(ang)

    def hidden(s, anchor=None):
        return _modnorm(x_ref[s * sub:(s + 1) * sub, :], g_ref[...], sc, sh, anchor).astype(BF16)

    def qk_norm_rot(zs, gain, tab, out_ref, rows):
        ss = jnp.dot((zs * zs).astype(BF16), gsum_ref[...], preferred_element_type=F32)
        y = zs * lax.rsqrt(ss * (1.0 / DA_HEAD_DIM) + EPS) * gain
        for j in range(DA_QK_W // LANES):
            sl = slice(j * LANES, (j + 1) * LANES)
            out_ref[rows, sl] = _rot128(y[:, sl], tab).astype(out_ref.dtype)

    z_next = jnp.dot(hidden(0), w_ref[...], preferred_element_type=F32)
    for s in range(nsub):
        rows = slice(s * sub, (s + 1) * sub)
        z = z_next
        if s + 1 < nsub:
            z_next = jnp.dot(hidden(s + 1, _zero_after(z[:, :1])), w_ref[...], preferred_element_type=F32)
        tab_a, tab_r = _rot_tables(cos_ref[rows, :], sin_ref[rows, :], (DA_HEAD_DIM // 2, 1))
        o = 0
        qk_norm_rot(z[:, o:o + DA_QK_W], gq_ref[...] * (DA_HEAD_DIM ** -0.5 * LOG2E), tab_a, qa_ref, rows)
        o += DA_QK_W
        qk_norm_rot(z[:, o:o + DA_QK_W], gk_ref[...], tab_a, ka_ref, rows)
        o += DA_QK_W
        vat_ref[0, :, rows] = z[:, o:o + DA_V_W].T.astype(vat_ref.dtype)
        o += DA_V_W
        for j in range(RET_QK_W // LANES):
            qr_ref[rows, j * LANES:(j + 1) * LANES] = _rot128(
                z[:, o + j * LANES:o + (j + 1) * LANES], tab_r).astype(qr_ref.dtype)
        o += RET_QK_W
        for j in range(RET_QK_W // LANES):
            kj = _rot128(z[:, o + j * LANES:o + (j + 1) * LANES], tab_r) * (RET_QK_DIM ** -0.5)
            kr_ref[rows, j * LANES:(j + 1) * LANES] = kj.astype(kr_ref.dtype)
            krt_ref[0, j * LANES:(j + 1) * LANES, rows] = kj.T.astype(krt_ref.dtype)
        o += RET_QK_W
        vr_ref[rows, :] = z[:, o:o + RET_V_W].astype(vr_ref.dtype)
        o += RET_V_W
        gate_ref[rows, :] = z[:, o:o + RET_V_W].astype(gate_ref.dtype)


def _proj(x, mod, g, pos, inv, w, gq, gk, gsum, *, seq, tm=1024, sub=256):
    t, d = x.shape
    n = PROJ_W
    tm = min(tm, seq)
    base = lax.broadcasted_iota(F32, (tm, LANES), 0) * inv
    bcos, bsin = jnp.cos(base), jnp.sin(base)

    def rows(width):
        return jax.ShapeDtypeStruct((t, width), BF16), pl.BlockSpec((tm, width), lambda i: (i, 0))

    def cols(width):
        return (jax.ShapeDtypeStruct((t // tm, width, tm), BF16),
                pl.BlockSpec((1, width, tm), lambda i: (i, 0, 0)))

    outs = [rows(DA_QK_W), rows(DA_QK_W), cols(DA_V_W), rows(RET_QK_W), rows(RET_QK_W), cols(RET_QK_W),
            rows(RET_V_W), rows(RET_V_W)]
    return pl.pallas_call(
        functools.partial(_proj_kernel, sub=sub),
        out_shape=[s for s, _ in outs],
        grid=(t // tm,),
        in_specs=[
            pl.BlockSpec((tm, d), lambda i: (i, 0)),
            pl.BlockSpec((1, N_MOD, d), lambda i: (i * tm // seq, 0, 0)),
            _resident((1, d)),
            pl.BlockSpec((tm, 1), lambda i: (i, 0)),
            _resident((1, LANES)),
            _resident((tm, LANES)),
            _resident((tm, LANES)),
            _resident((d, n)),
            _resident((1, DA_QK_W)),
            _resident((1, DA_QK_W)),
            _resident((DA_QK_W, DA_QK_W)),
        ],
        out_specs=[b for _, b in outs],
        scratch_shapes=[pltpu.VMEM((tm, LANES), F32)] * 2,
        compiler_params=_cparams(1),
        name="mix_proj",
    )(x, mod, g, pos, inv, bcos, bsin, w, gq, gk, gsum)


SUBLANES = 8


REDUCE_SLAB_ROWS = 8


def _reduce_rows(x, reduce_fn):
    r, n = x.shape
    if r % REDUCE_SLAB_ROWS == 0 and r > REDUCE_SLAB_ROWS:
        x = reduce_fn(x.reshape(r // REDUCE_SLAB_ROWS, REDUCE_SLAB_ROWS, n), axis=0)
    return reduce_fn(x, axis=0, keepdims=True)


def _slab_reduce(x, reduce_fn):
    r, n = x.shape
    return reduce_fn(x.reshape(r // REDUCE_SLAB_ROWS, REDUCE_SLAB_ROWS, n), axis=0)


def _dattn_kernel(q_ref, k_ref, vt_ref, lq1_ref, lk1_ref, lq2_ref, lk2_ref, gs_ref, o_ref, *,
                  lam_init, sub, kchunk):
    seq = k_ref.shape[0]
    nsub = q_ref.shape[0] // sub
    nch = seq // kchunk
    lam = (jnp.exp(jnp.sum(lq1_ref[...] * lk1_ref[...], axis=-1, keepdims=True))
           - jnp.exp(jnp.sum(lq2_ref[...] * lk2_ref[...], axis=-1, keepdims=True)) + lam_init)
    lane = lax.broadcasted_iota(jnp.int32, (sub, LANES), 1)

    def query_columns(t):
        q = q_ref[t * sub:(t + 1) * sub, :]
        zero = jnp.zeros_like(q)
        return jnp.concatenate([jnp.where(lane < DA_HEAD_DIM, q, zero),
                                jnp.where(lane >= DA_HEAD_DIM, q, zero)], axis=0)

    def finish(t, ot, l):
        o = (ot[:, :sub] * (1.0 / l[:, :sub]) - ot[:, sub:] * (lam / l[:, sub:])).T
        ms = jnp.mean(o * o, axis=-1, keepdims=True)
        o_ref[t * sub:(t + 1) * sub, :] = (
            o * lax.rsqrt(ms + EPS) * gs_ref[...] * (1.0 - lam_init)).astype(o_ref.dtype)

    st = m = p = l = None
    for it in range(nsub + 2):
        run_a, run_b, run_c = it < nsub, 1 <= it <= nsub, it >= 2
        qq = query_columns(it) if run_a else None
        st_new, p_new = [], []
        mx = ls = ot = None
        for c in range(nch):
            rows = slice(c * kchunk, (c + 1) * kchunk)
            if run_a:
                s_c = _dot_nt(k_ref[rows, :], qq)
                st_new.append(s_c)
                r = _slab_reduce(s_c, jnp.max)
                mx = r if mx is None else jnp.maximum(mx, r)
            if run_b:
                p_c = jnp.exp2(st[c] - m)
                r = _slab_reduce(p_c, jnp.sum)
                ls = r if ls is None else ls + r
                p_new.append(p_c.astype(BF16))
            if run_c:
                d = jnp.dot(_token_cols(vt_ref, rows), p[c], preferred_element_type=F32)
                ot = d if ot is None else ot + d
        if run_c:
            finish(it - 2, ot, l)
        if run_b:
            p, l = p_new, jnp.sum(ls, axis=0, keepdims=True)
        if run_a:
            st, m = st_new, jnp.max(mx, axis=0, keepdims=True)


def _dattn(qa, ka, vat, lq1, lk1, lq2, lk2, gs, *, bsz, seq, lam_init, tq=2048, sub=128, kchunk=512):
    t = qa.shape[0]
    tq = min(tq, seq)
    kchunk = min(kchunk, seq)
    nq = seq // tq
    lam_spec = _resident((1, DA_HEAD_DIM))
    return pl.pallas_call(
        functools.partial(_dattn_kernel, lam_init=lam_init, sub=sub, kchunk=kchunk),
        out_shape=jax.ShapeDtypeStruct((t, DA_V_W), BF16),
        grid=(bsz, DA_HEADS, nq),
        in_specs=[
            pl.BlockSpec((tq, LANES), lambda b, h, i: (b * nq + i, h)),
            pl.BlockSpec((seq, LANES), lambda b, h, i: (b, h)),
            pl.BlockSpec((seq // vat.shape[2], LANES, vat.shape[2]), lambda b, h, i: (b, h, 0)),
            lam_spec, lam_spec, lam_spec, lam_spec,
            _resident((1, DA_V_DIM)),
        ],
        out_specs=pl.BlockSpec((tq, LANES), lambda b, h, i: (b * nq + i, h)),
        compiler_params=_cparams(3),
        name="diff_attn",
    )(qa, ka, vat, lq1, lk1, lq2, lk2, gs)


def _ret_kernel(q_ref, k_ref, kt_ref, v_ref, gate_ref, df_ref, db_ref, gn_ref, o_ref,
                dmask_ref, zf_ref, xf_ref, zb_ref, xb_ref, *, chunk):
    hh = pl.program_id(0)
    seq = q_ref.shape[0]
    nchunk = seq // chunk
    hl = lax.broadcasted_iota(jnp.int32, df_ref.shape, 1) == hh

    def head_log_gamma(ref):
        lg = jax.nn.log_sigmoid(ref[...])
        return jnp.sum(jnp.where(hl, lg, 0.0), axis=-1, keepdims=True)

    lgf = head_log_gamma(df_ref)
    lgb = head_log_gamma(db_ref)

    @pl.when(pl.program_id(1) == 0)
    def _():
        a = lax.broadcasted_iota(jnp.int32, (chunk, chunk), 0)
        b = lax.broadcasted_iota(jnp.int32, (chunk, chunk), 1)
        rel = (a - b).astype(F32)
        dec = jnp.exp(jnp.where(rel >= 0, lgf, -lgb) * rel)
        dmask_ref[...] = jnp.where(rel == 0, 2.0, dec)
        r = lax.broadcasted_iota(jnp.int32, (chunk, LANES), 0).astype(F32)
        zf_ref[...] = jnp.exp((chunk - 1.0 - r) * lgf)
        xf_ref[...] = jnp.exp((r + 1.0) * lgf)
        zb_ref[...] = jnp.exp(r * lgb)
        xb_ref[...] = jnp.exp((chunk - r) * lgb)

    def rows(n):
        return slice(n * chunk, (n + 1) * chunk)

    kvf, kvb = [], []
    for n in range(nchunk):
        vn = v_ref[rows(n), :].astype(F32)
        ktn = _token_cols(kt_ref, rows(n))
        kvf.append(jnp.dot(ktn, (vn * zf_ref[...]).astype(BF16), preferred_element_type=F32))
        kvb.append(jnp.dot(ktn, (vn * zb_ref[...]).astype(BF16), preferred_element_type=F32))
    gcf = jnp.exp(chunk * lgf)
    gcb = jnp.exp(chunk * lgb)
    rf, rb = [None] * nchunk, [None] * nchunk
    r = jnp.zeros((LANES, RET_V_DIM), F32)
    for n in range(nchunk):
        rf[n] = r
        r = r * gcf + kvf[n]
    r = jnp.zeros((LANES, RET_V_DIM), F32)
    for n in reversed(range(nchunk)):
        rb[n] = r
        r = r * gcb + kvb[n]

    lane = lax.broadcasted_iota(jnp.int32, (chunk, LANES), 1)
    mine = (lane // RET_QK_DIM) == (hh % 2)
    for n in range(nchunk):
        qn = q_ref[rows(n), :]
        qn = jnp.where(mine, qn, jnp.zeros_like(qn))
        s = _dot_nt(qn, k_ref[rows(n), :])
        qf = qn.astype(F32)
        lhs = jnp.concatenate([(s * dmask_ref[...]).astype(BF16), (qf * xf_ref[...]).astype(BF16),
                               (qf * xb_ref[...]).astype(BF16)], axis=1)
        rhs = jnp.concatenate([v_ref[rows(n), :], rf[n].astype(BF16), rb[n].astype(BF16)], axis=0)
        y = jnp.dot(lhs, rhs, preferred_element_type=F32)
        ms = jnp.mean(y * y, axis=-1, keepdims=True)
        y = y * lax.rsqrt(ms + EPS) * gn_ref[...]
        o_ref[rows(n), :] = (y * _silu(gate_ref[rows(n), :].astype(F32))).astype(o_ref.dtype)


def _retention(qr, kr, krt, vr, gate, df, db, gn, *, bsz, seq, chunk=256):
    t = qr.shape[0]
    return pl.pallas_call(
        functools.partial(_ret_kernel, chunk=chunk),
        out_shape=jax.ShapeDtypeStruct((t, RET_V_W), BF16),
        grid=(RET_HEADS, bsz),
        in_specs=[
            pl.BlockSpec((seq, LANES), lambda h, b: (b, h // 2)),
            pl.BlockSpec((seq, LANES), lambda h, b: (b, h // 2)),
            pl.BlockSpec((seq // krt.shape[2], LANES, krt.shape[2]), lambda h, b: (b, h // 2, 0)),
            pl.BlockSpec((seq, LANES), lambda h, b: (b, h)),
            pl.BlockSpec((seq, LANES), lambda h, b: (b, h)),
            _resident((1, RET_HEADS)),
            _resident((1, RET_HEADS)),
            _resident((1, RET_V_DIM)),
        ],
        out_specs=pl.BlockSpec((seq, LANES), lambda h, b: (b, h)),
        scratch_shapes=[pltpu.VMEM((chunk, chunk), F32)] + [pltpu.VMEM((chunk, LANES), F32)] * 4,
        compiler_params=_cparams(2),
        name="retention",
    )(qr, kr, krt, vr, gate, df, db, gn)


def _merge_kernel(x_ref, mod_ref, g_ref, oa_ref, yr_ref, wma_ref, wmr_ref, bm_ref, wa_ref, wr_ref, wo_ref, o_ref,
                  *, sub):
    sh, sc, gt = _mod_rows(mod_ref, 3)
    nsub = x_ref.shape[0] // sub

    def hidden(s):
        return _modnorm(x_ref[s * sub:(s + 1) * sub, :], g_ref[...], sc, sh).astype(BF16)

    h = hidden(0)
    for s in range(nsub):
        rows = slice(s * sub, (s + 1) * sub)
        ga = jnp.dot(h, wma_ref[...], preferred_element_type=F32)
        gr = jnp.dot(h, wmr_ref[...], preferred_element_type=F32)
        h = hidden(s + 1) if s + 1 < nsub else None
        pa = jnp.dot(oa_ref[rows, :], wa_ref[...], preferred_element_type=F32)
        pr = jnp.dot(yr_ref[rows, :], wr_ref[...], preferred_element_type=F32)
        merged = jax.nn.sigmoid(ga + bm_ref[0:1, :]) * pa + jax.nn.sigmoid(gr + bm_ref[1:2, :]) * pr
        out = jnp.dot(merged.astype(BF16), wo_ref[...], preferred_element_type=F32)
        o_ref[rows, :] = x_ref[rows, :] + gt * out


def _merge(x, mod, g, oa, yr, w_in, bm, wa, wr, wo, *, seq, tm=1024, sub=256):
    t, d = x.shape
    tm = min(tm, seq)
    gate_blk = PROJ_W // d

    def gate_spec(j):
        return pl.BlockSpec((d, d), lambda i: (0, gate_blk + j), pipeline_mode=pl.Buffered(1))

    return pl.pallas_call(
        functools.partial(_merge_kernel, sub=sub),
        out_shape=jax.ShapeDtypeStruct((t, d), F32),
        grid=(t // tm,),
        in_specs=[
            pl.BlockSpec((tm, d), lambda i: (i, 0)),
            pl.BlockSpec((1, N_MOD, d), lambda i: (i * tm // seq, 0, 0)),
            _resident((1, d)),
            pl.BlockSpec((tm, DA_V_W), lambda i: (i, 0)),
            pl.BlockSpec((tm, RET_V_W), lambda i: (i, 0)),
            gate_spec(0),
            gate_spec(1),
            _resident((2, d)),
            _resident((DA_V_W, d)),
            _resident((RET_V_W, d)),
            _resident((d, d)),
        ],
        out_specs=pl.BlockSpec((tm, d), lambda i: (i, 0)),
        compiler_params=_cparams(1),
        name="mix_merge",
    )(x, mod, g, oa, yr, w_in, w_in, bm, wa, wr, wo)


def _head_freqs(inv_half, repeat):
    return jnp.tile(jnp.repeat(inv_half, repeat), 2 // repeat)


def kernel(x, c, positions, w_ada, b_ada, norm_ffn1, ffn1_w1, ffn1_w3, ffn1_w2, norm_mix, w_in, b_merge,
           da_q_gain, da_k_gain, da_lambda_q1, da_lambda_k1, da_lambda_q2, da_lambda_k2, da_subln,
           ret_decay_f, ret_decay_b, ret_norm, w_branch_a, w_branch_r, w_out, norm_ffn2, ffn2_w1, ffn2_w3,
           ffn2_w2):
    bsz, seq, d = x.shape
    depth = w_ada.shape[0]
    t = bsz * seq
    xt = x.reshape(t, d)
    pos = positions.reshape(t, 1)

    inv = jnp.concatenate([
        _head_freqs(1.0 / (ROPE_THETA ** (jnp.arange(0, DA_HEAD_DIM, 2, dtype=F32) / DA_HEAD_DIM)), 1),
        _head_freqs(1.0 / (ROPE_THETA ** jnp.linspace(0.0, 1.0, RET_QK_DIM // 2, dtype=F32)), 2),
    ]).reshape(1, LANES)
    n_groups = DA_QK_W // DA_HEAD_DIM
    gsum = jnp.kron(jnp.eye(n_groups, dtype=F32), jnp.ones((DA_HEAD_DIM, DA_HEAD_DIM), F32)).astype(BF16)

    for l in range(depth):
        lam_init = 0.8 - 0.6 * math.exp(-0.3 * l)
        mod = _ada(c, w_ada[l], b_ada[l].reshape(1, -1)).reshape(bsz, N_MOD, d)

        xt, (w_in_b, wa_b, wr_b, wo_b, f2w1_b, f2w3_b, f2w2_b) = _ffn(
            xt, mod, norm_ffn1[l].reshape(1, d), ffn1_w1[l].astype(BF16), ffn1_w3[l].astype(BF16),
            ffn1_w2[l].astype(BF16), k0=0, seq=seq,
            casts=(w_in[l], w_branch_a[l], w_branch_r[l], w_out[l], ffn2_w1[l], ffn2_w3[l], ffn2_w2[l]))

        qa, ka, vat, qr, kr, krt, vr, gate = _proj(
            xt, mod, norm_mix[l].reshape(1, d), pos, inv, w_in_b,
            jnp.tile(da_q_gain[l], n_groups).reshape(1, DA_QK_W),
            jnp.tile(da_k_gain[l], n_groups).reshape(1, DA_QK_W),
            gsum, seq=seq)

        oa = _dattn(qa, ka, vat, da_lambda_q1[l].reshape(1, -1), da_lambda_k1[l].reshape(1, -1),
                    da_lambda_q2[l].reshape(1, -1), da_lambda_k2[l].reshape(1, -1),
                    da_subln[l].reshape(1, -1), bsz=bsz, seq=seq, lam_init=lam_init)
        yr = _retention(qr, kr, krt, vr, gate, ret_decay_f[l].reshape(1, -1), ret_decay_b[l].reshape(1, -1),
                        ret_norm[l].reshape(1, -1), bsz=bsz, seq=seq)

        xt = _merge(xt, mod, norm_mix[l].reshape(1, d), oa, yr, w_in_b, b_merge[l], wa_b, wr_b, wo_b, seq=seq)

        xt, _ = _ffn(xt, mod, norm_ffn2[l].reshape(1, d), f2w1_b, f2w3_b, f2w2_b, k0=6, seq=seq)
    return xt.reshape(bsz, seq, d)
```

```python
import functools
import math

import jax
import jax.numpy as jnp
from jax import lax
from jax.experimental import pallas as pl
from jax.experimental.pallas import tpu as pltpu

F32 = jnp.float32
BF16 = jnp.bfloat16

DA_HEADS = 4
DA_HEAD_DIM = 64
DA_V_DIM = 2 * DA_HEAD_DIM
RET_HEADS = 4
RET_QK_DIM = 64
RET_V_DIM = 128
ROPE_THETA = 10000.0
EPS = 1e-6
N_MOD = 9
LOG2E = math.log2(math.e)

DA_QK_W = DA_HEADS * 2 * DA_HEAD_DIM
DA_V_W = DA_HEADS * DA_V_DIM
RET_QK_W = RET_HEADS * RET_QK_DIM
RET_V_W = RET_HEADS * RET_V_DIM
PROJ_W = 2 * DA_QK_W + DA_V_W + 2 * RET_QK_W + 2 * RET_V_W

LANES = 128
BF16_SUBLANES = 16
VMEM_LIMIT_BYTES = 56 * 1024 * 1024


def _cparams(n_axes, flags=None):
    return pltpu.CompilerParams(
        flags=flags,
        dimension_semantics=("arbitrary",) * n_axes,
        vmem_limit_bytes=VMEM_LIMIT_BYTES,
    )


def _resident(shape):
    nd = len(shape)
    return pl.BlockSpec(shape, lambda *_: (0,) * nd, pipeline_mode=pl.Buffered(1))


def _silu(a):
    return a * jax.nn.sigmoid(a)


def _zero_after(v):
    return ((pltpu.bitcast(v, jnp.uint32) >> 16) >> 16).astype(F32)


def _modnorm(x, g, sc, sh, anchor=None):
    ms = jnp.mean(x * x, axis=-1, keepdims=True)
    if anchor is not None:
        ms = ms + anchor
    y = x * lax.rsqrt(ms + EPS) * g
    return y * (1.0 + sc) + sh


def _mod_rows(mod_ref, k0):
    return (mod_ref[0, k0:k0 + 1, :], mod_ref[0, k0 + 1:k0 + 2, :], mod_ref[0, k0 + 2:k0 + 3, :])


def _dot_nt(a, b):
    return lax.dot_general(a, b, (((1,), (1,)), ((), ())), preferred_element_type=F32)


def _token_cols(ref, rows):
    tile = ref.shape[2]
    start, size = rows.start, rows.stop - rows.start
    assert start // tile == (start + size - 1) // tile
    return ref[start // tile, :, start % tile:start % tile + size]


def _ada_kernel(c_ref, w_ref, b_ref, o_ref):
    o_ref[...] = jnp.dot(_silu(c_ref[...]), w_ref[...], preferred_element_type=F32) + b_ref[...]


def _ada(c, w, b, tn=1024):
    bsz, d = c.shape
    n = w.shape[1]
    return pl.pallas_call(
        _ada_kernel,
        out_shape=jax.ShapeDtypeStruct((bsz, n), F32),
        grid=(n // tn,),
        in_specs=[
            pl.BlockSpec((bsz, d), lambda j: (0, 0)),
            pl.BlockSpec((d, tn), lambda j: (0, j)),
            pl.BlockSpec((1, tn), lambda j: (0, j)),
        ],
        out_specs=pl.BlockSpec((bsz, tn), lambda j: (0, j)),
        compiler_params=_cparams(1),
        name="ada_mod",
    )(c, w, b)


def _ffn_kernel(*refs, k0, sub, ncast):
    x_ref, mod_ref, g_ref, w1_ref, w3_ref, w2_ref = refs[:6]
    cast_in = refs[6:6 + ncast]
    o_ref = refs[6 + ncast]
    cast_out = refs[7 + ncast:]
    sh, sc, gt = _mod_rows(mod_ref, k0)
    nsub = x_ref.shape[0] // sub

    def hidden(s):
        return _modnorm(x_ref[s * sub:(s + 1) * sub, :], g_ref[...], sc, sh).astype(BF16)

    h = hidden(0)
    for s in range(nsub):
        rows = slice(s * sub, (s + 1) * sub)
        a = jnp.dot(h, w1_ref[...], preferred_element_type=F32)
        h_next = hidden(s + 1) if s + 1 < nsub else None
        b = jnp.dot(h, w3_ref[...], preferred_element_type=F32)
        u = (_silu(a) * b).astype(BF16)
        d = jnp.dot(u, w2_ref[...], preferred_element_type=F32)
        o_ref[rows, :] = x_ref[rows, :] + (0.5 * gt) * d
        h = h_next
    for src, dst in zip(cast_in, cast_out):
        dst[...] = src[...].astype(dst.dtype)


def _cast_slab_spec(rows, cols, nsteps):
    hold = 1
    while (rows * hold) % (nsteps * BF16_SUBLANES):
        hold *= 2
    slab = rows * hold // nsteps
    return pl.BlockSpec((slab, cols), lambda i: (i // hold, 0))


def _ffn(x, mod, g, w1, w3, w2, *, k0, seq, casts=(), tm=1024, sub=256):
    t, d = x.shape
    f = w1.shape[1]
    tm = min(tm, seq)
    nsteps = t // tm
    cast_specs = [_cast_slab_spec(w.shape[0], w.shape[1], nsteps) for w in casts]
    outs = pl.pallas_call(
        functools.partial(_ffn_kernel, k0=k0, sub=sub, ncast=len(casts)),
        out_shape=[jax.ShapeDtypeStruct((t, d), F32)] + [jax.ShapeDtypeStruct(w.shape, BF16) for w in casts],
        grid=(nsteps,),
        in_specs=[
            pl.BlockSpec((tm, d), lambda i: (i, 0)),
            pl.BlockSpec((1, N_MOD, d), lambda i: (i * tm // seq, 0, 0)),
            _resident((1, d)),
            _resident((d, f)),
            _resident((d, f)),
            _resident((f, d)),
        ] + cast_specs,
        out_specs=[pl.BlockSpec((tm, d), lambda i: (i, 0))] + cast_specs,
        compiler_params=_cparams(1),
        name=f"ffn_{k0}",
    )(x, mod, g, w1, w3, w2, *casts)
    return outs[0], outs[1:]


def _rot_tables(c, s, dists):
    c_swapped = pltpu.roll(c, LANES // 2, 1)
    s_swapped = pltpu.roll(s, LANES // 2, 1)
    lane = lax.broadcasted_iota(jnp.int32, (1, LANES), 1)
    low = lane < LANES // 2
    tabs = []
    for dist, (ck, sk) in zip(dists, ((jnp.where(low, c, c_swapped), jnp.where(low, s, s_swapped)),
                                      (jnp.where(low, c_swapped, c), jnp.where(low, s_swapped, s)))):
        first = (lane & dist) == 0
        tabs.append((ck, jnp.where(first, -sk, 0.0), jnp.where(first, 0.0, sk), dist))
    return tabs


def _rot128(y, tabs):
    c, sa, sb, dist = tabs
    return y * c + pltpu.roll(y, LANES - dist, 1) * sa + pltpu.roll(y, dist, 1) * sb


def _proj_kernel(x_ref, mod_ref, g_ref, pos_ref, inv_ref, bcos_ref, bsin_ref, w_ref, gq_ref, gk_ref, gsum_ref,
                 qa_ref, ka_ref, vat_ref, qr_ref, kr_ref, krt_ref, vr_ref, gate_ref, cos_ref, sin_ref, *, sub):
    sh, sc, _ = _mod_rows(mod_ref, 3)
    nsub = x_ref.shape[0] // sub

    pos = pos_ref[...]
    inv = inv_ref[...]
    ang0 = pos[0:1, :].astype(F32) * inv
    c0, s0 = jnp.cos(ang0), jnp.sin(ang0)
    cos_ref[...] = c0 * bcos_ref[...] - s0 * bsin_ref[...]
    sin_ref[...] = s0 * bcos_ref[...] + c0 * bsin_ref[...]
    step = pos - pos[0:1, :] - lax.broadcasted_iota(jnp.int32, pos.shape, 0)

    @pl.when(jnp.max(jnp.where(step != 0, 1.0, 0.0)) > 0.0)
    def _():
        ang = pos.astype(F32) * inv
        cos_ref[...] = jnp.cos(ang)
        sin_ref[...] = jnp.sin(ang)

    def hidden(s, anchor=None):
        return _modnorm(x_ref[s * sub:(s + 1) * sub, :], g_ref[...], sc, sh, anchor).astype(BF16)

    def qk_norm_rot(zs, gain, tab, out_ref, rows):
        ss = jnp.dot((zs * zs).astype(BF16), gsum_ref[...], preferred_element_type=F32)
        y = zs * lax.rsqrt(ss * (1.0 / DA_HEAD_DIM) + EPS) * gain
        for j in range(DA_QK_W // LANES):
            sl = slice(j * LANES, (j + 1) * LANES)
            out_ref[rows, sl] = _rot128(y[:, sl], tab).astype(out_ref.dtype)

    z_next = jnp.dot(hidden(0), w_ref[...], preferred_element_type=F32)
    for s in range(nsub):
        rows = slice(s * sub, (s + 1) * sub)
        z = z_next
        if s + 1 < nsub:
            z_next = jnp.dot(hidden(s + 1, _zero_after(z[:, :1])), w_ref[...], preferred_element_type=F32)
        tab_a, tab_r = _rot_tables(cos_ref[rows, :], sin_ref[rows, :], (DA_HEAD_DIM // 2, 1))
        o = 0
        qk_norm_rot(z[:, o:o + DA_QK_W], gq_ref[...] * (DA_HEAD_DIM ** -0.5 * LOG2E), tab_a, qa_ref, rows)
        o += DA_QK_W
        qk_norm_rot(z[:, o:o + DA_QK_W], gk_ref[...], tab_a, ka_ref, rows)
        o += DA_QK_W
        vat_ref[0, :, rows] = z[:, o:o + DA_V_W].T.astype(vat_ref.dtype)
        o += DA_V_W
        for j in range(RET_QK_W // LANES):
            qr_ref[rows, j * LANES:(j + 1) * LANES] = _rot128(
                z[:, o + j * LANES:o + (j + 1) * LANES], tab_r).astype(qr_ref.dtype)
        o += RET_QK_W
        for j in range(RET_QK_W // LANES):
            kj = _rot128(z[:, o + j * LANES:o + (j + 1) * LANES], tab_r) * (RET_QK_DIM ** -0.5)
            kr_ref[rows, j * LANES:(j + 1) * LANES] = kj.astype(kr_ref.dtype)
            krt_ref[0, j * LANES:(j + 1) * LANES, rows] = kj.T.astype(krt_ref.dtype)
        o += RET_QK_W
        vr_ref[rows, :] = z[:, o:o + RET_V_W].astype(vr_ref.dtype)
        o += RET_V_W
        gate_ref[rows, :] = z[:, o:o + RET_V_W].astype(gate_ref.dtype)


def _proj(x, mod, g, pos, inv, w, gq, gk, gsum, *, seq, tm=1024, sub=256):
    t, d = x.shape
    n = PROJ_W
    tm = min(tm, seq)
    base = lax.broadcasted_iota(F32, (tm, LANES), 0) * inv
    bcos, bsin = jnp.cos(base), jnp.sin(base)

    def rows(width):
        return jax.ShapeDtypeStruct((t, width), BF16), pl.BlockSpec((tm, width), lambda i: (i, 0))

    def cols(width):
        return (jax.ShapeDtypeStruct((t // tm, width, tm), BF16),
                pl.BlockSpec((1, width, tm), lambda i: (i, 0, 0)))

    outs = [rows(DA_QK_W), rows(DA_QK_W), cols(DA_V_W), rows(RET_QK_W), rows(RET_QK_W), cols(RET_QK_W),
            rows(RET_V_W), rows(RET_V_W)]
    return pl.pallas_call(
        functools.partial(_proj_kernel, sub=sub),
        out_shape=[s for s, _ in outs],
        grid=(t // tm,),
        in_specs=[
            pl.BlockSpec((tm, d), lambda i: (i, 0)),
            pl.BlockSpec((1, N_MOD, d), lambda i: (i * tm // seq, 0, 0)),
            _resident((1, d)),
            pl.BlockSpec((tm, 1), lambda i: (i, 0)),
            _resident((1, LANES)),
            _resident((tm, LANES)),
            _resident((tm, LANES)),
            _resident((d, n)),
            _resident((1, DA_QK_W)),
            _resident((1, DA_QK_W)),
            _resident((DA_QK_W, DA_QK_W)),
        ],
        out_specs=[b for _, b in outs],
        scratch_shapes=[pltpu.VMEM((tm, LANES), F32)] * 2,
        compiler_params=_cparams(1),
        name="mix_proj",
    )(x, mod, g, pos, inv, bcos, bsin, w, gq, gk, gsum)


SUBLANES = 8


REDUCE_SLAB_ROWS = 8
MAX_UNSHIFTED_LOG2_SCORE = 40.0


def _reduce_rows(x, reduce_fn):
    r, n = x.shape
    if r % REDUCE_SLAB_ROWS == 0 and r > REDUCE_SLAB_ROWS:
        x = reduce_fn(x.reshape(r // REDUCE_SLAB_ROWS, REDUCE_SLAB_ROWS, n), axis=0)
    return reduce_fn(x, axis=0, keepdims=True)


def _slab_reduce(x, reduce_fn):
    r, n = x.shape
    return reduce_fn(x.reshape(r // REDUCE_SLAB_ROWS, REDUCE_SLAB_ROWS, n), axis=0)


def _dattn_kernel(q_ref, k_ref, vt_ref, lq1_ref, lk1_ref, lq2_ref, lk2_ref, gs_ref, gq_ref, gk_ref, o_ref, *,
                  lam_init, sub, kchunk):
    seq = k_ref.shape[0]
    nsub = q_ref.shape[0] // sub
    nch = seq // kchunk
    lam = (jnp.exp(jnp.sum(lq1_ref[...] * lk1_ref[...], axis=-1, keepdims=True))
           - jnp.exp(jnp.sum(lq2_ref[...] * lk2_ref[...], axis=-1, keepdims=True)) + lam_init)
    lane = lax.broadcasted_iota(jnp.int32, (sub, LANES), 1)

    def query_columns(t):
        q = q_ref[t * sub:(t + 1) * sub, :]
        zero = jnp.zeros_like(q)
        return jnp.concatenate([jnp.where(lane < DA_HEAD_DIM, q, zero),
                                jnp.where(lane >= DA_HEAD_DIM, q, zero)], axis=0)

    def finish(t, ot, l):
        o = (ot[:, :sub] * (1.0 / l[:, :sub]) - ot[:, sub:] * (lam / l[:, sub:])).T
        ms = jnp.mean(o * o, axis=-1, keepdims=True)
        o_ref[t * sub:(t + 1) * sub, :] = (
            o * lax.rsqrt(ms + EPS) * gs_ref[...] * (1.0 - lam_init)).astype(o_ref.dtype)

    def shifted_softmax_path():
        st = m = p = l = None
        for it in range(nsub + 2):
            run_a, run_b, run_c = it < nsub, 1 <= it <= nsub, it >= 2
            qq = query_columns(it) if run_a else None
            st_new, p_new = [], []
            mx = ls = ot = None
            for c in range(nch):
                rows = slice(c * kchunk, (c + 1) * kchunk)
                if run_a:
                    s_c = _dot_nt(k_ref[rows, :], qq)
                    st_new.append(s_c)
                    r = _slab_reduce(s_c, jnp.max)
                    mx = r if mx is None else jnp.maximum(mx, r)
                if run_b:
                    p_c = jnp.exp2(st[c] - m)
                    r = _slab_reduce(p_c, jnp.sum)
                    ls = r if ls is None else ls + r
                    p_new.append(p_c.astype(BF16))
                if run_c:
                    d = jnp.dot(_token_cols(vt_ref, rows), p[c], preferred_element_type=F32)
                    ot = d if ot is None else ot + d
            if run_c:
                finish(it - 2, ot, l)
            if run_b:
                p, l = p_new, jnp.sum(ls, axis=0, keepdims=True)
            if run_a:
                st, m = st_new, jnp.max(mx, axis=0, keepdims=True)

    def unshifted_softmax_path():
        p = l = None
        for it in range(nsub + 1):
            run_a, run_c = it < nsub, it >= 1
            qq = query_columns(it) if run_a else None
            p_new = []
            ls = ot = None
            for c in range(nch):
                rows = slice(c * kchunk, (c + 1) * kchunk)
                if run_a:
                    p_c = jnp.exp2(_dot_nt(k_ref[rows, :], qq))
                    r = _slab_reduce(p_c, jnp.sum)
                    ls = r if ls is None else ls + r
                    p_new.append(p_c.astype(BF16))
                if run_c:
                    d = jnp.dot(_token_cols(vt_ref, rows), p[c], preferred_element_type=F32)
                    ot = d if ot is None else ot + d
            if run_c:
                finish(it - 1, ot, l)
            if run_a:
                p, l = p_new, jnp.sum(ls, axis=0, keepdims=True)

    bound = (jnp.max(jnp.abs(gq_ref[...])) * jnp.max(jnp.abs(gk_ref[...]))
             * (1.02 * DA_HEAD_DIM * DA_HEAD_DIM ** -0.5 * LOG2E))
    small_scores = bound <= MAX_UNSHIFTED_LOG2_SCORE
    pl.when(small_scores)(unshifted_softmax_path)
    pl.when(jnp.logical_not(small_scores))(shifted_softmax_path)


def _dattn(qa, ka, vat, lq1, lk1, lq2, lk2, gs, gq, gk, *, bsz, seq, lam_init, tq=2048, sub=128, kchunk=512):
    t = qa.shape[0]
    tq = min(tq, seq)
    kchunk = min(kchunk, seq)
    nq = seq // tq
    lam_spec = _resident((1, DA_HEAD_DIM))
    return pl.pallas_call(
        functools.partial(_dattn_kernel, lam_init=lam_init, sub=sub, kchunk=kchunk),
        out_shape=jax.ShapeDtypeStruct((t, DA_V_W), BF16),
        grid=(bsz, DA_HEADS, nq),
        in_specs=[
            pl.BlockSpec((tq, LANES), lambda b, h, i: (b * nq + i, h)),
            pl.BlockSpec((seq, LANES), lambda b, h, i: (b, h)),
            pl.BlockSpec((seq // vat.shape[2], LANES, vat.shape[2]), lambda b, h, i: (b, h, 0)),
            lam_spec, lam_spec, lam_spec, lam_spec,
            _resident((1, DA_V_DIM)),
            lam_spec, lam_spec,
        ],
        out_specs=pl.BlockSpec((tq, LANES), lambda b, h, i: (b * nq + i, h)),
        compiler_params=_cparams(3),
        name="diff_attn",
    )(qa, ka, vat, lq1, lk1, lq2, lk2, gs, gq, gk)


def _ret_kernel(q_ref, k_ref, kt_ref, v_ref, gate_ref, df_ref, db_ref, gn_ref, o_ref,
                dmask_ref, zf_ref, xf_ref, zb_ref, xb_ref, *, chunk):
    hh = pl.program_id(0)
    seq = q_ref.shape[0]
    nchunk = seq // chunk
    hl = lax.broadcasted_iota(jnp.int32, df_ref.shape, 1) == hh

    def head_log_gamma(ref):
        lg = jax.nn.log_sigmoid(ref[...])
        return jnp.sum(jnp.where(hl, lg, 0.0), axis=-1, keepdims=True)

    lgf = head_log_gamma(df_ref)
    lgb = head_log_gamma(db_ref)

    @pl.when(pl.program_id(1) == 0)
    def _():
        a = lax.broadcasted_iota(jnp.int32, (chunk, chunk), 0)
        b = lax.broadcasted_iota(jnp.int32, (chunk, chunk), 1)
        rel = (a - b).astype(F32)
        dec = jnp.exp(jnp.where(rel >= 0, lgf, -lgb) * rel)
        dmask_ref[...] = jnp.where(rel == 0, 2.0, dec)
        r = lax.broadcasted_iota(jnp.int32, (chunk, LANES), 0).astype(F32)
        zf_ref[...] = jnp.exp((chunk - 1.0 - r) * lgf)
        xf_ref[...] = jnp.exp((r + 1.0) * lgf)
        zb_ref[...] = jnp.exp(r * lgb)
        xb_ref[...] = jnp.exp((chunk - r) * lgb)

    def rows(n):
        return slice(n * chunk, (n + 1) * chunk)

    kvf, kvb = [], []
    for n in range(nchunk):
        vn = v_ref[rows(n), :].astype(F32)
        ktn = _token_cols(kt_ref, rows(n))
        kvf.append(jnp.dot(ktn, (vn * zf_ref[...]).astype(BF16), preferred_element_type=F32))
        kvb.append(jnp.dot(ktn, (vn * zb_ref[...]).astype(BF16), preferred_element_type=F32))
    gcf = jnp.exp(chunk * lgf)
    gcb = jnp.exp(chunk * lgb)
    rf, rb = [None] * nchunk, [None] * nchunk
    r = jnp.zeros((LANES, RET_V_DIM), F32)
    for n in range(nchunk):
        rf[n] = r
        r = r * gcf + kvf[n]
    r = jnp.zeros((LANES, RET_V_DIM), F32)
    for n in reversed(range(nchunk)):
        rb[n] = r
        r = r * gcb + kvb[n]

    lane = lax.broadcasted_iota(jnp.int32, (chunk, LANES), 1)
    mine = (lane // RET_QK_DIM) == (hh % 2)
    for n in range(nchunk):
        qn = q_ref[rows(n), :]
        qn = jnp.where(mine, qn, jnp.zeros_like(qn))
        s = _dot_nt(qn, k_ref[rows(n), :])
        qf = qn.astype(F32)
        lhs = jnp.concatenate([(s * dmask_ref[...]).astype(BF16), (qf * xf_ref[...]).astype(BF16),
                               (qf * xb_ref[...]).astype(BF16)], axis=1)
        rhs = jnp.concatenate([v_ref[rows(n), :], rf[n].astype(BF16), rb[n].astype(BF16)], axis=0)
        y = jnp.dot(lhs, rhs, preferred_element_type=F32)
        ms = jnp.mean(y * y, axis=-1, keepdims=True)
        y = y * lax.rsqrt(ms + EPS) * gn_ref[...]
        o_ref[rows(n), :] = (y * _silu(gate_ref[rows(n), :].astype(F32))).astype(o_ref.dtype)


def _retention(qr, kr, krt, vr, gate, df, db, gn, *, bsz, seq, chunk=256):
    t = qr.shape[0]
    return pl.pallas_call(
        functools.partial(_ret_kernel, chunk=chunk),
        out_shape=jax.ShapeDtypeStruct((t, RET_V_W), BF16),
        grid=(RET_HEADS, bsz),
        in_specs=[
            pl.BlockSpec((seq, LANES), lambda h, b: (b, h // 2)),
            pl.BlockSpec((seq, LANES), lambda h, b: (b, h // 2)),
            pl.BlockSpec((seq // krt.shape[2], LANES, krt.shape[2]), lambda h, b: (b, h // 2, 0)),
            pl.BlockSpec((seq, LANES), lambda h, b: (b, h)),
            pl.BlockSpec((seq, LANES), lambda h, b: (b, h)),
            _resident((1, RET_HEADS)),
            _resident((1, RET_HEADS)),
            _resident((1, RET_V_DIM)),
        ],
        out_specs=pl.BlockSpec((seq, LANES), lambda h, b: (b, h)),
        scratch_shapes=[pltpu.VMEM((chunk, chunk), F32)] + [pltpu.VMEM((chunk, LANES), F32)] * 4,
        compiler_params=_cparams(2),
        name="retention",
    )(qr, kr, krt, vr, gate, df, db, gn)


def _merge_kernel(x_ref, mod_ref, g_ref, oa_ref, yr_ref, wma_ref, wmr_ref, bm_ref, wa_ref, wr_ref, wo_ref, o_ref,
                  *, sub):
    sh, sc, gt = _mod_rows(mod_ref, 3)
    nsub = x_ref.shape[0] // sub

    def hidden(s):
        return _modnorm(x_ref[s * sub:(s + 1) * sub, :], g_ref[...], sc, sh).astype(BF16)

    h = hidden(0)
    for s in range(nsub):
        rows = slice(s * sub, (s + 1) * sub)
        ga = jnp.dot(h, wma_ref[...], preferred_element_type=F32)
        gr = jnp.dot(h, wmr_ref[...], preferred_element_type=F32)
        h = hidden(s + 1) if s + 1 < nsub else None
        pa = jnp.dot(oa_ref[rows, :], wa_ref[...], preferred_element_type=F32)
        pr = jnp.dot(yr_ref[rows, :], wr_ref[...], preferred_element_type=F32)
        merged = jax.nn.sigmoid(ga + bm_ref[0:1, :]) * pa + jax.nn.sigmoid(gr + bm_ref[1:2, :]) * pr
        out = jnp.dot(merged.astype(BF16), wo_ref[...], preferred_element_type=F32)
        o_ref[rows, :] = x_ref[rows, :] + gt * out


def _merge(x, mod, g, oa, yr, w_in, bm, wa, wr, wo, *, seq, tm=1024, sub=256):
    t, d = x.shape
    tm = min(tm, seq)
    gate_blk = PROJ_W // d

    def gate_spec(j):
        return pl.BlockSpec((d, d), lambda i: (0, gate_blk + j), pipeline_mode=pl.Buffered(1))

    return pl.pallas_call(
        functools.partial(_merge_kernel, sub=sub),
        out_shape=jax.ShapeDtypeStruct((t, d), F32),
        grid=(t // tm,),
        in_specs=[
            pl.BlockSpec((tm, d), lambda i: (i, 0)),
            pl.BlockSpec((1, N_MOD, d), lambda i: (i * tm // seq, 0, 0)),
            _resident((1, d)),
            pl.BlockSpec((tm, DA_V_W), lambda i: (i, 0)),
            pl.BlockSpec((tm, RET_V_W), lambda i: (i, 0)),
            gate_spec(0),
            gate_spec(1),
            _resident((2, d)),
            _resident((DA_V_W, d)),
            _resident((RET_V_W, d)),
            _resident((d, d)),
        ],
        out_specs=pl.BlockSpec((tm, d), lambda i: (i, 0)),
        compiler_params=_cparams(1),
        name="mix_merge",
    )(x, mod, g, oa, yr, w_in, w_in, bm, wa, wr, wo)


def _head_freqs(inv_half, repeat):
    return jnp.tile(jnp.repeat(inv_half, repeat), 2 // repeat)


def kernel(x, c, positions, w_ada, b_ada, norm_ffn1, ffn1_w1, ffn1_w3, ffn1_w2, norm_mix, w_in, b_merge,
           da_q_gain, da_k_gain, da_lambda_q1, da_lambda_k1, da_lambda_q2, da_lambda_k2, da_subln,
           ret_decay_f, ret_decay_b, ret_norm, w_branch_a, w_branch_r, w_out, norm_ffn2, ffn2_w1, ffn2_w3,
           ffn2_w2):
    bsz, seq, d = x.shape
    depth = w_ada.shape[0]
    t = bsz * seq
    xt = x.reshape(t, d)
    pos = positions.reshape(t, 1)

    inv = jnp.concatenate([
        _head_freqs(1.0 / (ROPE_THETA ** (jnp.arange(0, DA_HEAD_DIM, 2, dtype=F32) / DA_HEAD_DIM)), 1),
        _head_freqs(1.0 / (ROPE_THETA ** jnp.linspace(0.0, 1.0, RET_QK_DIM // 2, dtype=F32)), 2),
    ]).reshape(1, LANES)
    n_groups = DA_QK_W // DA_HEAD_DIM
    gsum = jnp.kron(jnp.eye(n_groups, dtype=F32), jnp.ones((DA_HEAD_DIM, DA_HEAD_DIM), F32)).astype(BF16)

    for l in range(depth):
        lam_init = 0.8 - 0.6 * math.exp(-0.3 * l)
        mod = _ada(c, w_ada[l], b_ada[l].reshape(1, -1)).reshape(bsz, N_MOD, d)

        xt, (w_in_b, wa_b, wr_b, wo_b, f2w1_b, f2w3_b, f2w2_b) = _ffn(
            xt, mod, norm_ffn1[l].reshape(1, d), ffn1_w1[l].astype(BF16), ffn1_w3[l].astype(BF16),
            ffn1_w2[l].astype(BF16), k0=0, seq=seq,
            casts=(w_in[l], w_branch_a[l], w_branch_r[l], w_out[l], ffn2_w1[l], ffn2_w3[l], ffn2_w2[l]))

        qa, ka, vat, qr, kr, krt, vr, gate = _proj(
            xt, mod, norm_mix[l].reshape(1, d), pos, inv, w_in_b,
            jnp.tile(da_q_gain[l], n_groups).reshape(1, DA_QK_W),
            jnp.tile(da_k_gain[l], n_groups).reshape(1, DA_QK_W),
            gsum, seq=seq)

        oa = _dattn(qa, ka, vat, da_lambda_q1[l].reshape(1, -1), da_lambda_k1[l].reshape(1, -1),
                    da_lambda_q2[l].reshape(1, -1), da_lambda_k2[l].reshape(1, -1),
                    da_subln[l].reshape(1, -1), da_q_gain[l].reshape(1, -1), da_k_gain[l].reshape(1, -1),
                    bsz=bsz, seq=seq, lam_init=lam_init)
        yr = _retention(qr, kr, krt, vr, gate, ret_decay_f[l].reshape(1, -1), ret_decay_b[l].reshape(1, -1),
                        ret_norm[l].reshape(1, -1), bsz=bsz, seq=seq)

        xt = _merge(xt, mod, norm_mix[l].reshape(1, d), oa, yr, w_in_b, b_merge[l], wa_b, wr_b, wo_b, seq=seq)

        xt, _ = _ffn(xt, mod, norm_ffn2[l].reshape(1, d), f2w1_b, f2w3_b, f2w2_b, k0=6, seq=seq)
    return xt.reshape(bsz, seq, d)
```

```python
import functools
import math

import jax
import jax.numpy as jnp
from jax import lax
from jax.experimental import pallas as pl
from jax.experimental.pallas import tpu as pltpu

F32 = jnp.float32
BF16 = jnp.bfloat16

DA_HEADS = 4
DA_HEAD_DIM = 64
DA_V_DIM = 2 * DA_HEAD_DIM
RET_HEADS = 4
RET_QK_DIM = 64
RET_V_DIM = 128
ROPE_THETA = 10000.0
EPS = 1e-6
N_MOD = 9
LOG2E = math.log2(math.e)

DA_QK_W = DA_HEADS * 2 * DA_HEAD_DIM
DA_V_W = DA_HEADS * DA_V_DIM
RET_QK_W = RET_HEADS * RET_QK_DIM
RET_V_W = RET_HEADS * RET_V_DIM
PROJ_W = 2 * DA_QK_W + DA_V_W + 2 * RET_QK_W + 2 * RET_V_W

LANES = 128
BF16_SUBLANES = 16
VMEM_LIMIT_BYTES = 56 * 1024 * 1024


def _cparams(n_axes, flags=None):
    return pltpu.CompilerParams(
        flags=flags,
        dimension_semantics=("arbitrary",) * n_axes,
        vmem_limit_bytes=VMEM_LIMIT_BYTES,
    )


def _resident(shape):
    nd = len(shape)
    return pl.BlockSpec(shape, lambda *_: (0,) * nd, pipeline_mode=pl.Buffered(1))


def _silu(a):
    return a * jax.nn.sigmoid(a)


def _zero_after(v):
    return ((pltpu.bitcast(v, jnp.uint32) >> 16) >> 16).astype(F32)


def _modnorm(x, g, sc, sh, anchor=None):
    ms = jnp.mean(x * x, axis=-1, keepdims=True)
    if anchor is not None:
        ms = ms + anchor
    y = x * lax.rsqrt(ms + EPS) * g
    return y * (1.0 + sc) + sh


def _mod_rows(mod_ref, k0):
    return (mod_ref[0, k0:k0 + 1, :], mod_ref[0, k0 + 1:k0 + 2, :], mod_ref[0, k0 + 2:k0 + 3, :])


def _dot_nt(a, b):
    return lax.dot_general(a, b, (((1,), (1,)), ((), ())), preferred_element_type=F32)


def _token_cols(ref, rows, features=slice(None)):
    tile = ref.shape[2]
    start, size = rows.start, rows.stop - rows.start
    assert start // tile == (start + size - 1) // tile
    return ref[start // tile, features, start % tile:start % tile + size]


def _ada_kernel(c_ref, w_ref, b_ref, o_ref):
    @pl.when(pl.program_id(0) == 0)
    def _():
        o_ref[...] = jnp.broadcast_to(b_ref[...], o_ref.shape)

    o_ref[...] += jnp.dot(_silu(c_ref[...]), w_ref[...], preferred_element_type=F32)


def _ada(c, w, b, tk=128):
    bsz, d = c.shape
    n = w.shape[1]
    return pl.pallas_call(
        _ada_kernel,
        out_shape=jax.ShapeDtypeStruct((bsz, n), F32),
        grid=(d // tk,),
        in_specs=[
            pl.BlockSpec((bsz, tk), lambda k: (0, k)),
            pl.BlockSpec((tk, n), lambda k: (k, 0)),
            _resident((1, n)),
        ],
        out_specs=pl.BlockSpec((bsz, n), lambda k: (0, 0)),
        compiler_params=_cparams(1),
        name="ada_mod",
    )(c, w, b)


def _ffn_kernel(*refs, k0, sub, ncast):
    x_ref, mod_ref, g_ref, w1_ref, w3_ref, w2_ref = refs[:6]
    cast_in = refs[6:6 + ncast]
    o_ref = refs[6 + ncast]
    cast_out = refs[7 + ncast:]
    sh, sc, gt = _mod_rows(mod_ref, k0)
    nsub = x_ref.shape[0] // sub

    def hidden(s):
        return _modnorm(x_ref[s * sub:(s + 1) * sub, :], g_ref[...], sc, sh).astype(BF16)

    h = hidden(0)
    for s in range(nsub):
        rows = slice(s * sub, (s + 1) * sub)
        a = jnp.dot(h, w1_ref[...], preferred_element_type=F32)
        h_next = hidden(s + 1) if s + 1 < nsub else None
        b = jnp.dot(h, w3_ref[...], preferred_element_type=F32)
        u = (_silu(a) * b).astype(BF16)
        d = jnp.dot(u, w2_ref[...], preferred_element_type=F32)
        o_ref[rows, :] = x_ref[rows, :] + (0.5 * gt) * d
        h = h_next
    for src, dst in zip(cast_in, cast_out):
        dst[...] = src[...].astype(dst.dtype)


def _cast_slab_spec(rows, cols, nsteps):
    hold = 1
    while (rows * hold) % (nsteps * BF16_SUBLANES):
        hold *= 2
    slab = rows * hold // nsteps
    return pl.BlockSpec((slab, cols), lambda i: (i // hold, 0))


def _ffn(x, mod, g, w1, w3, w2, *, k0, seq, casts=(), tm=1024, sub=256):
    t, d = x.shape
    f = w1.shape[1]
    tm = min(tm, seq)
    nsteps = t // tm
    cast_specs = [_cast_slab_spec(w.shape[0], w.shape[1], nsteps) for w in casts]
    outs = pl.pallas_call(
        functools.partial(_ffn_kernel, k0=k0, sub=sub, ncast=len(casts)),
        out_shape=[jax.ShapeDtypeStruct((t, d), F32)] + [jax.ShapeDtypeStruct(w.shape, BF16) for w in casts],
        grid=(nsteps,),
        in_specs=[
            pl.BlockSpec((tm, d), lambda i: (i, 0)),
            pl.BlockSpec((1, N_MOD, d), lambda i: (i * tm // seq, 0, 0)),
            _resident((1, d)),
            _resident((d, f)),
            _resident((d, f)),
            _resident((f, d)),
        ] + cast_specs,
        out_specs=[pl.BlockSpec((tm, d), lambda i: (i, 0))] + cast_specs,
        compiler_params=_cparams(1),
        name=f"ffn_{k0}",
    )(x, mod, g, w1, w3, w2, *casts)
    return outs[0], outs[1:]


def _rot_tables(c, s, dists):
    c_swapped = pltpu.roll(c, LANES // 2, 1)
    s_swapped = pltpu.roll(s, LANES // 2, 1)
    lane = lax.broadcasted_iota(jnp.int32, (1, LANES), 1)
    low = lane < LANES // 2
    tabs = []
    for dist, (ck, sk) in zip(dists, ((jnp.where(low, c, c_swapped), jnp.where(low, s, s_swapped)),
                                      (jnp.where(low, c_swapped, c), jnp.where(low, s_swapped, s)))):
        first = (lane & dist) == 0
        tabs.append((ck, jnp.where(first, -sk, 0.0), jnp.where(first, 0.0, sk), dist))
    return tabs


def _rot128(y, tabs):
    c, sa, sb, dist = tabs
    return y * c + pltpu.roll(y, LANES - dist, 1) * sa + pltpu.roll(y, dist, 1) * sb


def _proj_kernel(x_ref, mod_ref, g_ref, pos_ref, inv_ref, bcos_ref, bsin_ref, w_ref, gq_ref, gk_ref, gsum_ref,
                 qa_ref, ka_ref, vat_ref, qr_ref, kr_ref, krt_ref, vr_ref, gate_ref, cos_ref, sin_ref, *, sub):
    sh, sc, _ = _mod_rows(mod_ref, 3)
    nsub = x_ref.shape[0] // sub

    pos = pos_ref[...]
    inv = inv_ref[...]
    ang0 = pos[0:1, :].astype(F32) * inv
    c0, s0 = jnp.cos(ang0), jnp.sin(ang0)
    cos_ref[...] = c0 * bcos_ref[...] - s0 * bsin_ref[...]
    sin_ref[...] = s0 * bcos_ref[...] + c0 * bsin_ref[...]
    step = pos - pos[0:1, :] - lax.broadcasted_iota(jnp.int32, pos.shape, 0)

    @pl.when(jnp.max(jnp.where(step != 0, 1.0, 0.0)) > 0.0)
    def _():
        ang = pos.astype(F32) * inv
        cos_ref[...] = jnp.cos(ang)
        sin_ref[...] = jnp.sin(ang)

    def hidden(s, anchor=None):
        return _modnorm(x_ref[s * sub:(s + 1) * sub, :], g_ref[...], sc, sh, anchor).astype(BF16)

    def qk_norm_rot(zs, gain, tab, out_ref, rows):
        ss = jnp.dot((zs * zs).astype(BF16), gsum_ref[...], preferred_element_type=F32)
        y = zs * lax.rsqrt(ss * (1.0 / DA_HEAD_DIM) + EPS) * gain
        for j in range(DA_QK_W // LANES):
            sl = slice(j * LANES, (j + 1) * LANES)
            out_ref[rows, sl] = _rot128(y[:, sl], tab).astype(out_ref.dtype)

    z_next = jnp.dot(hidden(0), w_ref[...], preferred_element_type=F32)
    for s in range(nsub):
        rows = slice(s * sub, (s + 1) * sub)
        z = z_next
        if s + 1 < nsub:
            z_next = jnp.dot(hidden(s + 1, _zero_after(z[:, :1])), w_ref[...], preferred_element_type=F32)
        tab_a, tab_r = _rot_tables(cos_ref[rows, :], sin_ref[rows, :], (DA_HEAD_DIM // 2, 1))
        o = 0
        qk_norm_rot(z[:, o:o + DA_QK_W], gq_ref[...] * (DA_HEAD_DIM ** -0.5 * LOG2E), tab_a, qa_ref, rows)
        o += DA_QK_W
        qk_norm_rot(z[:, o:o + DA_QK_W], gk_ref[...], tab_a, ka_ref, rows)
        o += DA_QK_W
        vat_ref[0, :, rows] = z[:, o:o + DA_V_W].T.astype(vat_ref.dtype)
        o += DA_V_W
        for j in range(RET_QK_W // LANES):
            qr_ref[rows, j * LANES:(j + 1) * LANES] = _rot128(
                z[:, o + j * LANES:o + (j + 1) * LANES], tab_r).astype(qr_ref.dtype)
        o += RET_QK_W
        for j in range(RET_QK_W // LANES):
            kj = _rot128(z[:, o + j * LANES:o + (j + 1) * LANES], tab_r) * (RET_QK_DIM ** -0.5)
            kr_ref[rows, j * LANES:(j + 1) * LANES] = kj.astype(kr_ref.dtype)
            krt_ref[0, j * LANES:(j + 1) * LANES, rows] = kj.T.astype(krt_ref.dtype)
        o += RET_QK_W
        vr_ref[rows, :] = z[:, o:o + RET_V_W].astype(vr_ref.dtype)
        o += RET_V_W
        gate_ref[rows, :] = z[:, o:o + RET_V_W].astype(gate_ref.dtype)


def _proj(x, mod, g, pos, inv, w, gq, gk, gsum, *, seq, tm=1024, sub=256):
    t, d = x.shape
    n = PROJ_W
    tm = min(tm, seq)
    base = lax.broadcasted_iota(F32, (tm, LANES), 0) * inv
    bcos, bsin = jnp.cos(base), jnp.sin(base)

    def rows(width):
        return jax.ShapeDtypeStruct((t, width), BF16), pl.BlockSpec((tm, width), lambda i: (i, 0))

    def cols(width):
        return (jax.ShapeDtypeStruct((t // tm, width, tm), BF16),
                pl.BlockSpec((1, width, tm), lambda i: (i, 0, 0)))

    outs = [rows(DA_QK_W), rows(DA_QK_W), cols(DA_V_W), rows(RET_QK_W), rows(RET_QK_W), cols(RET_QK_W),
            rows(RET_V_W), rows(RET_V_W)]
    return pl.pallas_call(
        functools.partial(_proj_kernel, sub=sub),
        out_shape=[s for s, _ in outs],
        grid=(t // tm,),
        in_specs=[
            pl.BlockSpec((tm, d), lambda i: (i, 0)),
            pl.BlockSpec((1, N_MOD, d), lambda i: (i * tm // seq, 0, 0)),
            _resident((1, d)),
            pl.BlockSpec((tm, 1), lambda i: (i, 0)),
            _resident((1, LANES)),
            _resident((tm, LANES)),
            _resident((tm, LANES)),
            _resident((d, n)),
            _resident((1, DA_QK_W)),
            _resident((1, DA_QK_W)),
            _resident((DA_QK_W, DA_QK_W)),
        ],
        out_specs=[b for _, b in outs],
        scratch_shapes=[pltpu.VMEM((tm, LANES), F32)] * 2,
        compiler_params=_cparams(1),
        name="mix_proj",
    )(x, mod, g, pos, inv, bcos, bsin, w, gq, gk, gsum)


SUBLANES = 8


REDUCE_SLAB_ROWS = 8
MAX_UNSHIFTED_LOG2_SCORE = 40.0


def _reduce_rows(x, reduce_fn):
    r, n = x.shape
    if r % REDUCE_SLAB_ROWS == 0 and r > REDUCE_SLAB_ROWS:
        x = reduce_fn(x.reshape(r // REDUCE_SLAB_ROWS, REDUCE_SLAB_ROWS, n), axis=0)
    return reduce_fn(x, axis=0, keepdims=True)


def _slab_reduce(x, reduce_fn):
    r, n = x.shape
    return reduce_fn(x.reshape(r // REDUCE_SLAB_ROWS, REDUCE_SLAB_ROWS, n), axis=0)


def _dattn_kernel(q_ref, k_ref, vt_ref, lq1_ref, lk1_ref, lq2_ref, lk2_ref, gs_ref, gq_ref, gk_ref, o_ref, *,
                  lam_init, sub, kchunk):
    seq = k_ref.shape[0]
    nsub = q_ref.shape[0] // sub
    ntile = (q_ref.shape[1] // LANES) * nsub
    nch = seq // kchunk
    lam = (jnp.exp(jnp.sum(lq1_ref[...] * lk1_ref[...], axis=-1, keepdims=True))
           - jnp.exp(jnp.sum(lq2_ref[...] * lk2_ref[...], axis=-1, keepdims=True)) + lam_init)
    lane = lax.broadcasted_iota(jnp.int32, (sub, LANES), 1)

    def head_cols(tile):
        return slice((tile // nsub) * LANES, (tile // nsub + 1) * LANES)

    def tile_rows(tile):
        return slice((tile % nsub) * sub, (tile % nsub + 1) * sub)

    def query_columns(tile):
        q = q_ref[tile_rows(tile), head_cols(tile)]
        zero = jnp.zeros_like(q)
        return jnp.concatenate([jnp.where(lane < DA_HEAD_DIM, q, zero),
                                jnp.where(lane >= DA_HEAD_DIM, q, zero)], axis=0)

    def finish(tile, ot, l):
        o = (ot[:, :sub] * (1.0 / l[:, :sub]) - ot[:, sub:] * (lam / l[:, sub:])).T
        ms = jnp.mean(o * o, axis=-1, keepdims=True)
        o_ref[tile_rows(tile), head_cols(tile)] = (
            o * lax.rsqrt(ms + EPS) * gs_ref[...] * (1.0 - lam_init)).astype(o_ref.dtype)

    def shifted_softmax_path():
        st = m = p = l = None
        for it in range(ntile + 2):
            run_a, run_b, run_c = it < ntile, 1 <= it <= ntile, it >= 2
            qq = query_columns(it) if run_a else None
            st_new, p_new = [], []
            mx = ls = ot = None
            for c in range(nch):
                rows = slice(c * kchunk, (c + 1) * kchunk)
                if run_a:
                    s_c = _dot_nt(k_ref[rows, head_cols(it)], qq)
                    st_new.append(s_c)
                    r = _slab_reduce(s_c, jnp.max)
                    mx = r if mx is None else jnp.maximum(mx, r)
                if run_b:
                    p_c = jnp.exp2(st[c] - m)
                    r = _slab_reduce(p_c, jnp.sum)
                    ls = r if ls is None else ls + r
                    p_new.append(p_c.astype(BF16))
                if run_c:
                    d = jnp.dot(_token_cols(vt_ref, rows, head_cols(it - 2)), p[c],
                                preferred_element_type=F32)
                    ot = d if ot is None else ot + d
            if run_c:
                finish(it - 2, ot, l)
            if run_b:
                p, l = p_new, jnp.sum(ls, axis=0, keepdims=True)
            if run_a:
                st, m = st_new, jnp.max(mx, axis=0, keepdims=True)

    def unshifted_softmax_path():
        p = l = None
        for it in range(ntile + 1):
            run_a, run_c = it < ntile, it >= 1
            qq = query_columns(it) if run_a else None
            p_new = []
            ls = ot = None
            for c in range(nch):
                rows = slice(c * kchunk, (c + 1) * kchunk)
                if run_a:
                    p_c = jnp.exp2(_dot_nt(k_ref[rows, head_cols(it)], qq))
                    r = _slab_reduce(p_c, jnp.sum)
                    ls = r if ls is None else ls + r
                    p_new.append(p_c.astype(BF16))
                if run_c:
                    d = jnp.dot(_token_cols(vt_ref, rows, head_cols(it - 1)), p[c],
                                preferred_element_type=F32)
                    ot = d if ot is None else ot + d
            if run_c:
                finish(it - 1, ot, l)
            if run_a:
                p, l = p_new, jnp.sum(ls, axis=0, keepdims=True)

    bound = (jnp.max(jnp.abs(gq_ref[...])) * jnp.max(jnp.abs(gk_ref[...]))
             * (1.02 * DA_HEAD_DIM * DA_HEAD_DIM ** -0.5 * LOG2E))
    small_scores = bound <= MAX_UNSHIFTED_LOG2_SCORE
    pl.when(small_scores)(unshifted_softmax_path)
    pl.when(jnp.logical_not(small_scores))(shifted_softmax_path)


def _dattn(qa, ka, vat, lq1, lk1, lq2, lk2, gs, gq, gk, *, bsz, seq, lam_init, tq=2048, sub=128, kchunk=512,
           heads_per_step=1):
    t = qa.shape[0]
    tq = min(tq, seq)
    kchunk = min(kchunk, seq)
    nq = seq // tq
    width = heads_per_step * LANES
    lam_spec = _resident((1, DA_HEAD_DIM))
    return pl.pallas_call(
        functools.partial(_dattn_kernel, lam_init=lam_init, sub=sub, kchunk=kchunk),
        out_shape=jax.ShapeDtypeStruct((t, DA_V_W), BF16),
        grid=(bsz, DA_HEADS // heads_per_step, nq),
        in_specs=[
            pl.BlockSpec((tq, width), lambda b, h, i: (b * nq + i, h)),
            pl.BlockSpec((seq, width), lambda b, h, i: (b, h)),
            pl.BlockSpec((seq // vat.shape[2], width, vat.shape[2]), lambda b, h, i: (b, h, 0)),
            lam_spec, lam_spec, lam_spec, lam_spec,
            _resident((1, DA_V_DIM)),
            lam_spec, lam_spec,
        ],
        out_specs=pl.BlockSpec((tq, width), lambda b, h, i: (b * nq + i, h)),
        compiler_params=_cparams(3),
        name="diff_attn",
    )(qa, ka, vat, lq1, lk1, lq2, lk2, gs, gq, gk)


def _ret_kernel(q_ref, k_ref, kt_ref, v_ref, gate_ref, df_ref, db_ref, gn_ref, o_ref,
                dmask_ref, zf_ref, xf_ref, zb_ref, xb_ref, *, chunk):
    hh = pl.program_id(0)
    seq = q_ref.shape[0]
    nchunk = seq // chunk
    hl = lax.broadcasted_iota(jnp.int32, df_ref.shape, 1) == hh

    def head_log_gamma(ref):
        lg = jax.nn.log_sigmoid(ref[...])
        return jnp.sum(jnp.where(hl, lg, 0.0), axis=-1, keepdims=True)

    lgf = head_log_gamma(df_ref)
    lgb = head_log_gamma(db_ref)

    @pl.when(pl.program_id(1) == 0)
    def _():
        a = lax.broadcasted_iota(jnp.int32, (chunk, chunk), 0)
        b = lax.broadcasted_iota(jnp.int32, (chunk, chunk), 1)
        rel = (a - b).astype(F32)
        dec = jnp.exp(jnp.where(rel >= 0, lgf, -lgb) * rel)
        dmask_ref[...] = jnp.where(rel == 0, 2.0, dec)
        r = lax.broadcasted_iota(jnp.int32, (chunk, LANES), 0).astype(F32)
        zf_ref[...] = jnp.exp((chunk - 1.0 - r) * lgf)
        xf_ref[...] = jnp.exp((r + 1.0) * lgf)
        zb_ref[...] = jnp.exp(r * lgb)
        xb_ref[...] = jnp.exp((chunk - r) * lgb)

    def rows(n):
        return slice(n * chunk, (n + 1) * chunk)

    lane = lax.broadcasted_iota(jnp.int32, (chunk, LANES), 1)
    mine = (lane // RET_QK_DIM) == (hh % 2)

    def scores(n):
        qn = q_ref[rows(n), :]
        qn = jnp.where(mine, qn, jnp.zeros_like(qn))
        return qn, _dot_nt(qn, k_ref[rows(n), :])

    ahead = [scores(0)]
    kvf, kvb = [], []
    for n in range(nchunk):
        vn = v_ref[rows(n), :].astype(F32)
        ktn = _token_cols(kt_ref, rows(n))
        kvf.append(jnp.dot(ktn, (vn * zf_ref[...]).astype(BF16), preferred_element_type=F32))
        kvb.append(jnp.dot(ktn, (vn * zb_ref[...]).astype(BF16), preferred_element_type=F32))
    gcf = jnp.exp(chunk * lgf)
    gcb = jnp.exp(chunk * lgb)
    rf, rb = [None] * nchunk, [None] * nchunk
    r = jnp.zeros((LANES, RET_V_DIM), F32)
    for n in range(nchunk):
        rf[n] = r
        r = r * gcf + kvf[n]
    r = jnp.zeros((LANES, RET_V_DIM), F32)
    for n in reversed(range(nchunk)):
        rb[n] = r
        r = r * gcb + kvb[n]

    for n in range(nchunk):
        if n + 1 < nchunk:
            ahead.append(scores(n + 1))
        qn, s = ahead[n]
        qf = qn.astype(F32)
        lhs = jnp.concatenate([(s * dmask_ref[...]).astype(BF16), (qf * xf_ref[...]).astype(BF16),
                               (qf * xb_ref[...]).astype(BF16)], axis=1)
        rhs = jnp.concatenate([v_ref[rows(n), :], rf[n].astype(BF16), rb[n].astype(BF16)], axis=0)
        y = jnp.dot(lhs, rhs, preferred_element_type=F32)
        ms = jnp.mean(y * y, axis=-1, keepdims=True)
        y = y * lax.rsqrt(ms + EPS) * gn_ref[...]
        o_ref[rows(n), :] = (y * _silu(gate_ref[rows(n), :].astype(F32))).astype(o_ref.dtype)


def _retention(qr, kr, krt, vr, gate, df, db, gn, *, bsz, seq, chunk=256):
    t = qr.shape[0]
    return pl.pallas_call(
        functools.partial(_ret_kernel, chunk=chunk),
        out_shape=jax.ShapeDtypeStruct((t, RET_V_W), BF16),
        grid=(RET_HEADS, bsz),
        in_specs=[
            pl.BlockSpec((seq, LANES), lambda h, b: (b, h // 2)),
            pl.BlockSpec((seq, LANES), lambda h, b: (b, h // 2)),
            pl.BlockSpec((seq // krt.shape[2], LANES, krt.shape[2]), lambda h, b: (b, h // 2, 0)),
            pl.BlockSpec((seq, LANES), lambda h, b: (b, h)),
            pl.BlockSpec((seq, LANES), lambda h, b: (b, h)),
            _resident((1, RET_HEADS)),
            _resident((1, RET_HEADS)),
            _resident((1, RET_V_DIM)),
        ],
        out_specs=pl.BlockSpec((seq, LANES), lambda h, b: (b, h)),
        scratch_shapes=[pltpu.VMEM((chunk, chunk), F32)] + [pltpu.VMEM((chunk, LANES), F32)] * 4,
        compiler_params=_cparams(2),
        name="retention",
    )(qr, kr, krt, vr, gate, df, db, gn)


def _merge_kernel(x_ref, mod_ref, g_ref, oa_ref, yr_ref, wma_ref, wmr_ref, bm_ref, wa_ref, wr_ref, wo_ref, o_ref,
                  *, sub):
    sh, sc, gt = _mod_rows(mod_ref, 3)
    nsub = x_ref.shape[0] // sub

    def hidden(s):
        return _modnorm(x_ref[s * sub:(s + 1) * sub, :], g_ref[...], sc, sh).astype(BF16)

    h = hidden(0)
    for s in range(nsub):
        rows = slice(s * sub, (s + 1) * sub)
        ga = jnp.dot(h, wma_ref[...], preferred_element_type=F32)
        gr = jnp.dot(h, wmr_ref[...], preferred_element_type=F32)
        h = hidden(s + 1) if s + 1 < nsub else None
        pa = jnp.dot(oa_ref[rows, :], wa_ref[...], preferred_element_type=F32)
        pr = jnp.dot(yr_ref[rows, :], wr_ref[...], preferred_element_type=F32)
        merged = jax.nn.sigmoid(ga + bm_ref[0:1, :]) * pa + jax.nn.sigmoid(gr + bm_ref[1:2, :]) * pr
        out = jnp.dot(merged.astype(BF16), wo_ref[...], preferred_element_type=F32)
        o_ref[rows, :] = x_ref[rows, :] + gt * out


def _merge(x, mod, g, oa, yr, w_in, bm, wa, wr, wo, *, seq, tm=1024, sub=256):
    t, d = x.shape
    tm = min(tm, seq)
    gate_blk = PROJ_W // d

    def gate_spec(j):
        return pl.BlockSpec((d, d), lambda i: (0, gate_blk + j), pipeline_mode=pl.Buffered(1))

    return pl.pallas_call(
        functools.partial(_merge_kernel, sub=sub),
        out_shape=jax.ShapeDtypeStruct((t, d), F32),
        grid=(t // tm,),
        in_specs=[
            pl.BlockSpec((tm, d), lambda i: (i, 0)),
            pl.BlockSpec((1, N_MOD, d), lambda i: (i * tm // seq, 0, 0)),
            _resident((1, d)),
            pl.BlockSpec((tm, DA_V_W), lambda i: (i, 0)),
            pl.BlockSpec((tm, RET_V_W), lambda i: (i, 0)),
            gate_spec(0),
            gate_spec(1),
            _resident((2, d)),
            _resident((DA_V_W, d)),
            _resident((RET_V_W, d)),
            _resident((d, d)),
        ],
        out_specs=pl.BlockSpec((tm, d), lambda i: (i, 0)),
        compiler_params=_cparams(1),
        name="mix_merge",
    )(x, mod, g, oa, yr, w_in, w_in, bm, wa, wr, wo)


def _head_freqs(inv_half, repeat):
    return jnp.tile(jnp.repeat(inv_half, repeat), 2 // repeat)


def kernel(x, c, positions, w_ada, b_ada, norm_ffn1, ffn1_w1, ffn1_w3, ffn1_w2, norm_mix, w_in, b_merge,
           da_q_gain, da_k_gain, da_lambda_q1, da_lambda_k1, da_lambda_q2, da_lambda_k2, da_subln,
           ret_decay_f, ret_decay_b, ret_norm, w_branch_a, w_branch_r, w_out, norm_ffn2, ffn2_w1, ffn2_w3,
           ffn2_w2):
    bsz, seq, d = x.shape
    depth = w_ada.shape[0]
    t = bsz * seq
    xt = x.reshape(t, d)
    pos = positions.reshape(t, 1)

    inv = jnp.concatenate([
        _head_freqs(1.0 / (ROPE_THETA ** (jnp.arange(0, DA_HEAD_DIM, 2, dtype=F32) / DA_HEAD_DIM)), 1),
        _head_freqs(1.0 / (ROPE_THETA ** jnp.linspace(0.0, 1.0, RET_QK_DIM // 2, dtype=F32)), 2),
    ]).reshape(1, LANES)
    n_groups = DA_QK_W // DA_HEAD_DIM
    gsum = jnp.kron(jnp.eye(n_groups, dtype=F32), jnp.ones((DA_HEAD_DIM, DA_HEAD_DIM), F32)).astype(BF16)

    for l in range(depth):
        lam_init = 0.8 - 0.6 * math.exp(-0.3 * l)
        mod = _ada(c, w_ada[l], b_ada[l].reshape(1, -1)).reshape(bsz, N_MOD, d)

        xt, (w_in_b, wa_b, wr_b, wo_b, f2w1_b, f2w3_b, f2w2_b) = _ffn(
            xt, mod, norm_ffn1[l].reshape(1, d), ffn1_w1[l].astype(BF16), ffn1_w3[l].astype(BF16),
            ffn1_w2[l].astype(BF16), k0=0, seq=seq,
            casts=(w_in[l], w_branch_a[l], w_branch_r[l], w_out[l], ffn2_w1[l], ffn2_w3[l], ffn2_w2[l]))

        qa, ka, vat, qr, kr, krt, vr, gate = _proj(
            xt, mod, norm_mix[l].reshape(1, d), pos, inv, w_in_b,
            jnp.tile(da_q_gain[l], n_groups).reshape(1, DA_QK_W),
            jnp.tile(da_k_gain[l], n_groups).reshape(1, DA_QK_W),
            gsum, seq=seq)

        oa = _dattn(qa, ka, vat, da_lambda_q1[l].reshape(1, -1), da_lambda_k1[l].reshape(1, -1),
                    da_lambda_q2[l].reshape(1, -1), da_lambda_k2[l].reshape(1, -1),
                    da_subln[l].reshape(1, -1), da_q_gain[l].reshape(1, -1), da_k_gain[l].reshape(1, -1),
                    bsz=bsz, seq=seq, lam_init=lam_init)
        yr = _retention(qr, kr, krt, vr, gate, ret_decay_f[l].reshape(1, -1), ret_decay_b[l].reshape(1, -1),
                        ret_norm[l].reshape(1, -1), bsz=bsz, seq=seq)

        xt = _merge(xt, mod, norm_mix[l].reshape(1, d), oa, yr, w_in_b, b_merge[l], wa_b, wr_b, wo_b, seq=seq)

        xt, _ = _ffn(xt, mod, norm_ffn2[l].reshape(1, d), f2w1_b, f2w3_b, f2w2_b, k0=6, seq=seq)
    return xt.reshape(bsz, seq, d)
```

```python
import functools
import math

import jax
import jax.numpy as jnp
from jax import lax
from jax.experimental import pallas as pl
from jax.experimental.pallas import tpu as pltpu

F32 = jnp.float32
BF16 = jnp.bfloat16

DA_HEADS = 4
DA_HEAD_DIM = 64
DA_V_DIM = 2 * DA_HEAD_DIM
RET_HEADS = 4
RET_QK_DIM = 64
RET_V_DIM = 128
ROPE_THETA = 10000.0
EPS = 1e-6
N_MOD = 9
LOG2E = math.log2(math.e)

DA_QK_W = DA_HEADS * 2 * DA_HEAD_DIM
DA_V_W = DA_HEADS * DA_V_DIM
RET_QK_W = RET_HEADS * RET_QK_DIM
RET_V_W = RET_HEADS * RET_V_DIM
PROJ_W = 2 * DA_QK_W + DA_V_W + 2 * RET_QK_W + 2 * RET_V_W

LANES = 128
BF16_SUBLANES = 16
VMEM_LIMIT_BYTES = 56 * 1024 * 1024


def _cparams(n_axes):
    return pltpu.CompilerParams(
        dimension_semantics=("arbitrary",) * n_axes,
        vmem_limit_bytes=VMEM_LIMIT_BYTES,
    )


def _resident(shape):
    nd = len(shape)
    return pl.BlockSpec(shape, lambda *_: (0,) * nd, pipeline_mode=pl.Buffered(1))


def _silu(a):
    return a * jax.nn.sigmoid(a)


def _zero_after(v):
    return ((pltpu.bitcast(v, jnp.uint32) >> 16) >> 16).astype(F32)


def _modnorm(x, g, sc, sh, anchor=None):
    ms = jnp.mean(x * x, axis=-1, keepdims=True)
    if anchor is not None:
        ms = ms + anchor
    y = x * lax.rsqrt(ms + EPS) * g
    return y * (1.0 + sc) + sh


def _mod_rows(mod_ref, k0):
    return (mod_ref[0, k0:k0 + 1, :], mod_ref[0, k0 + 1:k0 + 2, :], mod_ref[0, k0 + 2:k0 + 3, :])


def _dot_nt(a, b):
    return lax.dot_general(a, b, (((1,), (1,)), ((), ())), preferred_element_type=F32)


def _token_cols(ref, rows, features=slice(None)):
    tile = ref.shape[2]
    start, size = rows.start, rows.stop - rows.start
    assert start // tile == (start + size - 1) // tile
    return ref[start // tile, features, start % tile:start % tile + size]


def _ada_kernel(c_ref, w_ref, b_ref, o_ref):
    @pl.when(pl.program_id(0) == 0)
    def _():
        o_ref[...] = jnp.broadcast_to(b_ref[...], o_ref.shape)

    o_ref[...] += jnp.dot(_silu(c_ref[...]), w_ref[...], preferred_element_type=F32)


def _ada(c, w, b, tk=128):
    bsz, d = c.shape
    n = w.shape[1]
    return pl.pallas_call(
        _ada_kernel,
        out_shape=jax.ShapeDtypeStruct((bsz, n), F32),
        grid=(d // tk,),
        in_specs=[
            pl.BlockSpec((bsz, tk), lambda k: (0, k)),
            pl.BlockSpec((tk, n), lambda k: (k, 0)),
            _resident((1, n)),
        ],
        out_specs=pl.BlockSpec((bsz, n), lambda k: (0, 0)),
        compiler_params=_cparams(1),
        name="ada_mod",
    )(c, w, b)


def _ffn_kernel(*refs, k0, sub, ncast):
    x_ref, mod_ref, g_ref, w1_ref, w3_ref, w2_ref = refs[:6]
    cast_in = refs[6:6 + ncast]
    o_ref = refs[6 + ncast]
    cast_out = refs[7 + ncast:]
    sh, sc, gt = _mod_rows(mod_ref, k0)
    nsub = x_ref.shape[0] // sub

    def hidden(s):
        return _modnorm(x_ref[s * sub:(s + 1) * sub, :], g_ref[...], sc, sh).astype(BF16)

    h = hidden(0)
    for s in range(nsub):
        rows = slice(s * sub, (s + 1) * sub)
        a = jnp.dot(h, w1_ref[...], preferred_element_type=F32)
        h_next = hidden(s + 1) if s + 1 < nsub else None
        b = jnp.dot(h, w3_ref[...], preferred_element_type=F32)
        u = (_silu(a) * b).astype(BF16)
        d = jnp.dot(u, w2_ref[...], preferred_element_type=F32)
        o_ref[rows, :] = x_ref[rows, :] + (0.5 * gt) * d
        h = h_next
    for src, dst in zip(cast_in, cast_out):
        dst[...] = src[...].astype(dst.dtype)


def _cast_slab_spec(rows, cols, nsteps):
    hold = 1
    while (rows * hold) % (nsteps * BF16_SUBLANES):
        hold *= 2
    slab = rows * hold // nsteps
    return pl.BlockSpec((slab, cols), lambda i: (i // hold, 0))


def _ffn(x, mod, g, w1, w3, w2, *, k0, seq, casts=(), tm=1024, sub=256):
    t, d = x.shape
    f = w1.shape[1]
    tm = min(tm, seq)
    nsteps = t // tm
    cast_specs = [_cast_slab_spec(w.shape[0], w.shape[1], nsteps) for w in casts]
    outs = pl.pallas_call(
        functools.partial(_ffn_kernel, k0=k0, sub=sub, ncast=len(casts)),
        out_shape=[jax.ShapeDtypeStruct((t, d), F32)] + [jax.ShapeDtypeStruct(w.shape, BF16) for w in casts],
        grid=(nsteps,),
        in_specs=[
            pl.BlockSpec((tm, d), lambda i: (i, 0)),
            pl.BlockSpec((1, N_MOD, d), lambda i: (i * tm // seq, 0, 0)),
            _resident((1, d)),
            _resident((d, f)),
            _resident((d, f)),
            _resident((f, d)),
        ] + cast_specs,
        out_specs=[pl.BlockSpec((tm, d), lambda i: (i, 0))] + cast_specs,
        compiler_params=_cparams(1),
        name=f"ffn_{k0}",
    )(x, mod, g, w1, w3, w2, *casts)
    return outs[0], outs[1:]


def _rot_tables(c, s, dists):
    c_swapped = pltpu.roll(c, LANES // 2, 1)
    s_swapped = pltpu.roll(s, LANES // 2, 1)
    lane = lax.broadcasted_iota(jnp.int32, (1, LANES), 1)
    low = lane < LANES // 2
    tabs = []
    for dist, (ck, sk) in zip(dists, ((jnp.where(low, c, c_swapped), jnp.where(low, s, s_swapped)),
                                      (jnp.where(low, c_swapped, c), jnp.where(low, s_swapped, s)))):
        first = (lane & dist) == 0
        tabs.append((ck, jnp.where(first, -sk, 0.0), jnp.where(first, 0.0, sk), dist))
    return tabs


def _rot128(y, tabs):
    c, sa, sb, dist = tabs
    return y * c + pltpu.roll(y, LANES - dist, 1) * sa + pltpu.roll(y, dist, 1) * sb


def _proj_kernel(x_ref, mod_ref, g_ref, pos_ref, inv_ref, bcos_ref, bsin_ref, w_ref, gq_ref, gk_ref, gsum_ref,
                 qa_ref, ka_ref, vat_ref, qr_ref, kr_ref, krt_ref, vr_ref, gate_ref, cos_ref, sin_ref, *, sub):
    sh, sc, _ = _mod_rows(mod_ref, 3)
    nsub = x_ref.shape[0] // sub

    pos = pos_ref[...]
    inv = inv_ref[...]
    ang0 = pos[0:1, :].astype(F32) * inv
    c0, s0 = jnp.cos(ang0), jnp.sin(ang0)
    cos_ref[...] = c0 * bcos_ref[...] - s0 * bsin_ref[...]
    sin_ref[...] = s0 * bcos_ref[...] + c0 * bsin_ref[...]
    step = pos - pos[0:1, :] - lax.broadcasted_iota(jnp.int32, pos.shape, 0)

    @pl.when(jnp.max(jnp.where(step != 0, 1.0, 0.0)) > 0.0)
    def _():
        ang = pos.astype(F32) * inv
        cos_ref[...] = jnp.cos(ang)
        sin_ref[...] = jnp.sin(ang)

    def hidden(s, anchor=None):
        return _modnorm(x_ref[s * sub:(s + 1) * sub, :], g_ref[...], sc, sh, anchor).astype(BF16)

    def qk_norm_rot(zs, gain, tab, out_ref, rows):
        ss = jnp.dot((zs * zs).astype(BF16), gsum_ref[...], preferred_element_type=F32)
        y = zs * lax.rsqrt(ss * (1.0 / DA_HEAD_DIM) + EPS) * gain
        for j in range(DA_QK_W // LANES):
            sl = slice(j * LANES, (j + 1) * LANES)
            out_ref[rows, sl] = _rot128(y[:, sl], tab).astype(out_ref.dtype)

    z_next = jnp.dot(hidden(0), w_ref[...], preferred_element_type=F32)
    for s in range(nsub):
        rows = slice(s * sub, (s + 1) * sub)
        z = z_next
        if s + 1 < nsub:
            z_next = jnp.dot(hidden(s + 1, _zero_after(z[:, :1])), w_ref[...], preferred_element_type=F32)
        tab_a, tab_r = _rot_tables(cos_ref[rows, :], sin_ref[rows, :], (DA_HEAD_DIM // 2, 1))
        o = 0
        qk_norm_rot(z[:, o:o + DA_QK_W], gq_ref[...] * (DA_HEAD_DIM ** -0.5 * LOG2E), tab_a, qa_ref, rows)
        o += DA_QK_W
        qk_norm_rot(z[:, o:o + DA_QK_W], gk_ref[...], tab_a, ka_ref, rows)
        o += DA_QK_W
        vat_ref[0, :, rows] = z[:, o:o + DA_V_W].T.astype(vat_ref.dtype)
        o += DA_V_W
        for j in range(RET_QK_W // LANES):
            qr_ref[rows, j * LANES:(j + 1) * LANES] = _rot128(
                z[:, o + j * LANES:o + (j + 1) * LANES], tab_r).astype(qr_ref.dtype)
        o += RET_QK_W
        for j in range(RET_QK_W // LANES):
            kj = _rot128(z[:, o + j * LANES:o + (j + 1) * LANES], tab_r) * (RET_QK_DIM ** -0.5)
            kr_ref[rows, j * LANES:(j + 1) * LANES] = kj.astype(kr_ref.dtype)
            krt_ref[0, j * LANES:(j + 1) * LANES, rows] = kj.T.astype(krt_ref.dtype)
        o += RET_QK_W
        vr_ref[rows, :] = z[:, o:o + RET_V_W].astype(vr_ref.dtype)
        o += RET_V_W
        gate_ref[rows, :] = z[:, o:o + RET_V_W].astype(gate_ref.dtype)


def _proj(x, mod, g, pos, inv, w, gq, gk, gsum, *, seq, tm=1024, sub=256):
    t, d = x.shape
    n = PROJ_W
    tm = min(tm, seq)
    base = lax.broadcasted_iota(F32, (tm, LANES), 0) * inv
    bcos, bsin = jnp.cos(base), jnp.sin(base)

    def rows(width):
        return jax.ShapeDtypeStruct((t, width), BF16), pl.BlockSpec((tm, width), lambda i: (i, 0))

    def cols(width):
        return (jax.ShapeDtypeStruct((t // tm, width, tm), BF16),
                pl.BlockSpec((1, width, tm), lambda i: (i, 0, 0)))

    outs = [rows(DA_QK_W), rows(DA_QK_W), cols(DA_V_W), rows(RET_QK_W), rows(RET_QK_W), cols(RET_QK_W),
            rows(RET_V_W), rows(RET_V_W)]
    return pl.pallas_call(
        functools.partial(_proj_kernel, sub=sub),
        out_shape=[s for s, _ in outs],
        grid=(t // tm,),
        in_specs=[
            pl.BlockSpec((tm, d), lambda i: (i, 0)),
            pl.BlockSpec((1, N_MOD, d), lambda i: (i * tm // seq, 0, 0)),
            _resident((1, d)),
            pl.BlockSpec((tm, 1), lambda i: (i, 0)),
            _resident((1, LANES)),
            _resident((tm, LANES)),
            _resident((tm, LANES)),
            _resident((d, n)),
            _resident((1, DA_QK_W)),
            _resident((1, DA_QK_W)),
            _resident((DA_QK_W, DA_QK_W)),
        ],
        out_specs=[b for _, b in outs],
        scratch_shapes=[pltpu.VMEM((tm, LANES), F32)] * 2,
        compiler_params=_cparams(1),
        name="mix_proj",
    )(x, mod, g, pos, inv, bcos, bsin, w, gq, gk, gsum)


REDUCE_SLAB_ROWS = 8
MAX_UNSHIFTED_LOG2_SCORE = 40.0


def _slab_reduce(x, reduce_fn):
    r, n = x.shape
    return reduce_fn(x.reshape(r // REDUCE_SLAB_ROWS, REDUCE_SLAB_ROWS, n), axis=0)


def _dattn_kernel(q_ref, k_ref, vt_ref, lq1_ref, lk1_ref, lq2_ref, lk2_ref, gs_ref, gq_ref, gk_ref, o_ref, *,
                  lam_init, sub, kchunk):
    seq = k_ref.shape[0]
    nsub = q_ref.shape[0] // sub
    ntile = (q_ref.shape[1] // LANES) * nsub
    nch = seq // kchunk
    lam = (jnp.exp(jnp.sum(lq1_ref[...] * lk1_ref[...], axis=-1, keepdims=True))
           - jnp.exp(jnp.sum(lq2_ref[...] * lk2_ref[...], axis=-1, keepdims=True)) + lam_init)
    lane = lax.broadcasted_iota(jnp.int32, (sub, LANES), 1)

    def head_cols(tile):
        return slice((tile // nsub) * LANES, (tile // nsub + 1) * LANES)

    def tile_rows(tile):
        return slice((tile % nsub) * sub, (tile % nsub + 1) * sub)

    def query_columns(tile):
        q = q_ref[tile_rows(tile), head_cols(tile)]
        zero = jnp.zeros_like(q)
        return jnp.concatenate([jnp.where(lane < DA_HEAD_DIM, q, zero),
                                jnp.where(lane >= DA_HEAD_DIM, q, zero)], axis=0)

    def finish(tile, ot, l):
        o = (ot[:, :sub] * (1.0 / l[:, :sub]) - ot[:, sub:] * (lam / l[:, sub:])).T
        ms = jnp.mean(o * o, axis=-1, keepdims=True)
        o_ref[tile_rows(tile), head_cols(tile)] = (
            o * lax.rsqrt(ms + EPS) * gs_ref[...] * (1.0 - lam_init)).astype(o_ref.dtype)

    def shifted_softmax_path():
        st = m = p = l = None
        for it in range(ntile + 2):
            run_a, run_b, run_c = it < ntile, 1 <= it <= ntile, it >= 2
            qq = query_columns(it) if run_a else None
            st_new, p_new = [], []
            mx = ls = ot = None
            for c in range(nch):
                rows = slice(c * kchunk, (c + 1) * kchunk)
                if run_a:
                    s_c = _dot_nt(k_ref[rows, head_cols(it)], qq)
                    st_new.append(s_c)
                    r = _slab_reduce(s_c, jnp.max)
                    mx = r if mx is None else jnp.maximum(mx, r)
                if run_b:
                    p_c = jnp.exp2(st[c] - m)
                    r = _slab_reduce(p_c, jnp.sum)
                    ls = r if ls is None else ls + r
                    p_new.append(p_c.astype(BF16))
                if run_c:
                    d = jnp.dot(_token_cols(vt_ref, rows, head_cols(it - 2)), p[c],
                                preferred_element_type=F32)
                    ot = d if ot is None else ot + d
            if run_c:
                finish(it - 2, ot, l)
            if run_b:
                p, l = p_new, jnp.sum(ls, axis=0, keepdims=True)
            if run_a:
                st, m = st_new, jnp.max(mx, axis=0, keepdims=True)

    def unshifted_softmax_path():
        p = l = None
        for it in range(ntile + 1):
            run_a, run_c = it < ntile, it >= 1
            qq = query_columns(it) if run_a else None
            p_new = []
            ls = ot = None
            for c in range(nch):
                rows = slice(c * kchunk, (c + 1) * kchunk)
                if run_a:
                    p_c = jnp.exp2(_dot_nt(k_ref[rows, head_cols(it)], qq))
                    r = _slab_reduce(p_c, jnp.sum)
                    ls = r if ls is None else ls + r
                    p_new.append(p_c.astype(BF16))
                if run_c:
                    d = jnp.dot(_token_cols(vt_ref, rows, head_cols(it - 1)), p[c],
                                preferred_element_type=F32)
                    ot = d if ot is None else ot + d
            if run_c:
                finish(it - 1, ot, l)
            if run_a:
                p, l = p_new, jnp.sum(ls, axis=0, keepdims=True)

    bound = (jnp.max(jnp.abs(gq_ref[...])) * jnp.max(jnp.abs(gk_ref[...]))
             * (1.02 * DA_HEAD_DIM * DA_HEAD_DIM ** -0.5 * LOG2E))
    small_scores = bound <= MAX_UNSHIFTED_LOG2_SCORE
    pl.when(small_scores)(unshifted_softmax_path)
    pl.when(jnp.logical_not(small_scores))(shifted_softmax_path)


def _dattn(qa, ka, vat, lq1, lk1, lq2, lk2, gs, gq, gk, *, bsz, seq, lam_init, tq=2048, sub=128, kchunk=512,
           heads_per_step=1):
    t = qa.shape[0]
    tq = min(tq, seq)
    kchunk = min(kchunk, seq)
    nq = seq // tq
    width = heads_per_step * LANES
    lam_spec = _resident((1, DA_HEAD_DIM))
    return pl.pallas_call(
        functools.partial(_dattn_kernel, lam_init=lam_init, sub=sub, kchunk=kchunk),
        out_shape=jax.ShapeDtypeStruct((t, DA_V_W), BF16),
        grid=(bsz, DA_HEADS // heads_per_step, nq),
        in_specs=[
            pl.BlockSpec((tq, width), lambda b, h, i: (b * nq + i, h)),
            pl.BlockSpec((seq, width), lambda b, h, i: (b, h)),
            pl.BlockSpec((seq // vat.shape[2], width, vat.shape[2]), lambda b, h, i: (b, h, 0)),
            lam_spec, lam_spec, lam_spec, lam_spec,
            _resident((1, DA_V_DIM)),
            lam_spec, lam_spec,
        ],
        out_specs=pl.BlockSpec((tq, width), lambda b, h, i: (b * nq + i, h)),
        compiler_params=_cparams(3),
        name="diff_attn",
    )(qa, ka, vat, lq1, lk1, lq2, lk2, gs, gq, gk)


def _ret_kernel(q_ref, k_ref, kt_ref, v_ref, gate_ref, df_ref, db_ref, gn_ref, o_ref,
                dmask_ref, zf_ref, xf_ref, zb_ref, xb_ref, *, chunk, seq):
    hh = pl.program_id(0)
    nchunk = q_ref.shape[0] // chunk
    per_seq = seq // chunk
    hl = lax.broadcasted_iota(jnp.int32, df_ref.shape, 1) == hh

    def head_log_gamma(ref):
        lg = jax.nn.log_sigmoid(ref[...])
        return jnp.sum(jnp.where(hl, lg, 0.0), axis=-1, keepdims=True)

    lgf = head_log_gamma(df_ref)
    lgb = head_log_gamma(db_ref)

    @pl.when(pl.program_id(1) == 0)
    def _():
        a = lax.broadcasted_iota(jnp.int32, (chunk, chunk), 0)
        b = lax.broadcasted_iota(jnp.int32, (chunk, chunk), 1)
        rel = (a - b).astype(F32)
        dec = jnp.exp(jnp.where(rel >= 0, lgf, -lgb) * rel)
        dmask_ref[...] = jnp.where(rel == 0, 2.0, dec)
        r = lax.broadcasted_iota(jnp.int32, (chunk, LANES), 0).astype(F32)
        zf_ref[...] = jnp.exp((chunk - 1.0 - r) * lgf)
        xf_ref[...] = jnp.exp((r + 1.0) * lgf)
        zb_ref[...] = jnp.exp(r * lgb)
        xb_ref[...] = jnp.exp((chunk - r) * lgb)

    def rows(n):
        return slice(n * chunk, (n + 1) * chunk)

    lane = lax.broadcasted_iota(jnp.int32, (chunk, LANES), 1)
    mine = (lane // RET_QK_DIM) == (hh % 2)

    def scores(n):
        qn = q_ref[rows(n), :]
        qn = jnp.where(mine, qn, jnp.zeros_like(qn))
        return qn, _dot_nt(qn, k_ref[rows(n), :])

    ahead = [scores(0)]
    kvf, kvb = [], []
    for n in range(nchunk):
        vn = v_ref[rows(n), :].astype(F32)
        ktn = _token_cols(kt_ref, rows(n))
        kvf.append(jnp.dot(ktn, (vn * zf_ref[...]).astype(BF16), preferred_element_type=F32))
        kvb.append(jnp.dot(ktn, (vn * zb_ref[...]).astype(BF16), preferred_element_type=F32))
    gcf = jnp.exp(chunk * lgf)
    gcb = jnp.exp(chunk * lgb)
    rf, rb = [None] * nchunk, [None] * nchunk
    for first in range(0, nchunk, per_seq):
        r = jnp.zeros((LANES, RET_V_DIM), F32)
        for n in range(first, first + per_seq):
            rf[n] = r
            r = r * gcf + kvf[n]
        r = jnp.zeros((LANES, RET_V_DIM), F32)
        for n in reversed(range(first, first + per_seq)):
            rb[n] = r
            r = r * gcb + kvb[n]

    for n in range(nchunk):
        if n + 1 < nchunk:
            ahead.append(scores(n + 1))
        qn, s = ahead[n]
        qf = qn.astype(F32)
        lhs = jnp.concatenate([(s * dmask_ref[...]).astype(BF16), (qf * xf_ref[...]).astype(BF16),
                               (qf * xb_ref[...]).astype(BF16)], axis=1)
        rhs = jnp.concatenate([v_ref[rows(n), :], rf[n].astype(BF16), rb[n].astype(BF16)], axis=0)
        y = jnp.dot(lhs, rhs, preferred_element_type=F32)
        ms = jnp.mean(y * y, axis=-1, keepdims=True)
        y = y * lax.rsqrt(ms + EPS) * gn_ref[...]
        o_ref[rows(n), :] = (y * _silu(gate_ref[rows(n), :].astype(F32))).astype(o_ref.dtype)


def _retention(qr, kr, krt, vr, gate, df, db, gn, *, bsz, seq, chunk=256, seqs_per_step=4):
    t = qr.shape[0]
    seqs_per_step = math.gcd(seqs_per_step, bsz)
    rows = seqs_per_step * seq
    return pl.pallas_call(
        functools.partial(_ret_kernel, chunk=chunk, seq=seq),
        out_shape=jax.ShapeDtypeStruct((t, RET_V_W), BF16),
        grid=(RET_HEADS, bsz // seqs_per_step),
        in_specs=[
            pl.BlockSpec((rows, LANES), lambda h, b: (b, h // 2)),
            pl.BlockSpec((rows, LANES), lambda h, b: (b, h // 2)),
            pl.BlockSpec((rows // krt.shape[2], LANES, krt.shape[2]), lambda h, b: (b, h // 2, 0)),
            pl.BlockSpec((rows, LANES), lambda h, b: (b, h)),
            pl.BlockSpec((rows, LANES), lambda h, b: (b, h)),
            _resident((1, RET_HEADS)),
            _resident((1, RET_HEADS)),
            _resident((1, RET_V_DIM)),
        ],
        out_specs=pl.BlockSpec((rows, LANES), lambda h, b: (b, h)),
        scratch_shapes=[pltpu.VMEM((chunk, chunk), F32)] + [pltpu.VMEM((chunk, LANES), F32)] * 4,
        compiler_params=_cparams(2),
        name="retention",
    )(qr, kr, krt, vr, gate, df, db, gn)


def _merge_kernel(x_ref, mod_ref, g_ref, oa_ref, yr_ref, wma_ref, wmr_ref, bm_ref, wa_ref, wr_ref, wo_ref, o_ref,
                  *, sub):
    sh, sc, gt = _mod_rows(mod_ref, 3)
    nsub = x_ref.shape[0] // sub

    def hidden(s):
        return _modnorm(x_ref[s * sub:(s + 1) * sub, :], g_ref[...], sc, sh).astype(BF16)

    h = hidden(0)
    for s in range(nsub):
        rows = slice(s * sub, (s + 1) * sub)
        ga = jnp.dot(h, wma_ref[...], preferred_element_type=F32)
        gr = jnp.dot(h, wmr_ref[...], preferred_element_type=F32)
        h = hidden(s + 1) if s + 1 < nsub else None
        pa = jnp.dot(oa_ref[rows, :], wa_ref[...], preferred_element_type=F32)
        pr = jnp.dot(yr_ref[rows, :], wr_ref[...], preferred_element_type=F32)
        merged = jax.nn.sigmoid(ga + bm_ref[0:1, :]) * pa + jax.nn.sigmoid(gr + bm_ref[1:2, :]) * pr
        out = jnp.dot(merged.astype(BF16), wo_ref[...], preferred_element_type=F32)
        o_ref[rows, :] = x_ref[rows, :] + gt * out


def _merge(x, mod, g, oa, yr, w_in, bm, wa, wr, wo, *, seq, tm=1024, sub=256):
    t, d = x.shape
    tm = min(tm, seq)
    gate_blk = PROJ_W // d

    def gate_spec(j):
        return pl.BlockSpec((d, d), lambda i: (0, gate_blk + j), pipeline_mode=pl.Buffered(1))

    return pl.pallas_call(
        functools.partial(_merge_kernel, sub=sub),
        out_shape=jax.ShapeDtypeStruct((t, d), F32),
        grid=(t // tm,),
        in_specs=[
            pl.BlockSpec((tm, d), lambda i: (i, 0)),
            pl.BlockSpec((1, N_MOD, d), lambda i: (i * tm // seq, 0, 0)),
            _resident((1, d)),
            pl.BlockSpec((tm, DA_V_W), lambda i: (i, 0)),
            pl.BlockSpec((tm, RET_V_W), lambda i: (i, 0)),
            gate_spec(0),
            gate_spec(1),
            _resident((2, d)),
            _resident((DA_V_W, d)),
            _resident((RET_V_W, d)),
            _resident((d, d)),
        ],
        out_specs=pl.BlockSpec((tm, d), lambda i: (i, 0)),
        compiler_params=_cparams(1),
        name="mix_merge",
    )(x, mod, g, oa, yr, w_in, w_in, bm, wa, wr, wo)


def _head_freqs(inv_half, repeat):
    return jnp.tile(jnp.repeat(inv_half, repeat), 2 // repeat)


def kernel(x, c, positions, w_ada, b_ada, norm_ffn1, ffn1_w1, ffn1_w3, ffn1_w2, norm_mix, w_in, b_merge,
           da_q_gain, da_k_gain, da_lambda_q1, da_lambda_k1, da_lambda_q2, da_lambda_k2, da_subln,
           ret_decay_f, ret_decay_b, ret_norm, w_branch_a, w_branch_r, w_out, norm_ffn2, ffn2_w1, ffn2_w3,
           ffn2_w2):
    bsz, seq, d = x.shape
    depth = w_ada.shape[0]
    t = bsz * seq
    xt = x.reshape(t, d)
    pos = positions.reshape(t, 1)

    inv = jnp.concatenate([
        _head_freqs(1.0 / (ROPE_THETA ** (jnp.arange(0, DA_HEAD_DIM, 2, dtype=F32) / DA_HEAD_DIM)), 1),
        _head_freqs(1.0 / (ROPE_THETA ** jnp.linspace(0.0, 1.0, RET_QK_DIM // 2, dtype=F32)), 2),
    ]).reshape(1, LANES)
    n_groups = DA_QK_W // DA_HEAD_DIM
    gsum = jnp.kron(jnp.eye(n_groups, dtype=F32), jnp.ones((DA_HEAD_DIM, DA_HEAD_DIM), F32)).astype(BF16)

    for l in range(depth):
        lam_init = 0.8 - 0.6 * math.exp(-0.3 * l)
        mod = _ada(c, w_ada[l], b_ada[l].reshape(1, -1)).reshape(bsz, N_MOD, d)

        xt, (w_in_b, wa_b, wr_b, wo_b, f2w1_b, f2w3_b, f2w2_b) = _ffn(
            xt, mod, norm_ffn1[l].reshape(1, d), ffn1_w1[l].astype(BF16), ffn1_w3[l].astype(BF16),
            ffn1_w2[l].astype(BF16), k0=0, seq=seq,
            casts=(w_in[l], w_branch_a[l], w_branch_r[l], w_out[l], ffn2_w1[l], ffn2_w3[l], ffn2_w2[l]))

        qa, ka, vat, qr, kr, krt, vr, gate = _proj(
            xt, mod, norm_mix[l].reshape(1, d), pos, inv, w_in_b,
            jnp.tile(da_q_gain[l], n_groups).reshape(1, DA_QK_W),
            jnp.tile(da_k_gain[l], n_groups).reshape(1, DA_QK_W),
            gsum, seq=seq)

        oa = _dattn(qa, ka, vat, da_lambda_q1[l].reshape(1, -1), da_lambda_k1[l].reshape(1, -1),
                    da_lambda_q2[l].reshape(1, -1), da_lambda_k2[l].reshape(1, -1),
                    da_subln[l].reshape(1, -1), da_q_gain[l].reshape(1, -1), da_k_gain[l].reshape(1, -1),
                    bsz=bsz, seq=seq, lam_init=lam_init)
        yr = _retention(qr, kr, krt, vr, gate, ret_decay_f[l].reshape(1, -1), ret_decay_b[l].reshape(1, -1),
                        ret_norm[l].reshape(1, -1), bsz=bsz, seq=seq)

        xt = _merge(xt, mod, norm_mix[l].reshape(1, d), oa, yr, w_in_b, b_merge[l], wa_b, wr_b, wo_b, seq=seq)

        xt, _ = _ffn(xt, mod, norm_ffn2[l].reshape(1, d), f2w1_b, f2w3_b, f2w2_b, k0=6, seq=seq)
    return xt.reshape(bsz, seq, d)
```

```python
import functools
import math

import jax
import jax.numpy as jnp
from jax import lax
from jax.experimental import pallas as pl
from jax.experimental.pallas import tpu as pltpu

F32 = jnp.float32
BF16 = jnp.bfloat16

DA_HEADS = 4
DA_HEAD_DIM = 64
DA_V_DIM = 2 * DA_HEAD_DIM
RET_HEADS = 4
RET_QK_DIM = 64
RET_V_DIM = 128
ROPE_THETA = 10000.0
EPS = 1e-6
N_MOD = 9
LOG2E = math.log2(math.e)

DA_QK_W = DA_HEADS * 2 * DA_HEAD_DIM
DA_V_W = DA_HEADS * DA_V_DIM
RET_QK_W = RET_HEADS * RET_QK_DIM
RET_V_W = RET_HEADS * RET_V_DIM
PROJ_W = 2 * DA_QK_W + DA_V_W + 2 * RET_QK_W + 2 * RET_V_W

LANES = 128
BF16_SUBLANES = 16
VMEM_LIMIT_BYTES = 56 * 1024 * 1024


def _cparams(n_axes):
    return pltpu.CompilerParams(
        dimension_semantics=("arbitrary",) * n_axes,
        vmem_limit_bytes=VMEM_LIMIT_BYTES,
    )


def _resident(shape):
    nd = len(shape)
    return pl.BlockSpec(shape, lambda *_: (0,) * nd, pipeline_mode=pl.Buffered(1))


def _silu(a):
    return a * jax.nn.sigmoid(a)


def _zero_after(v):
    return ((pltpu.bitcast(v, jnp.uint32) >> 16) >> 16).astype(F32)


def _modnorm(x, g, sc, sh, anchor=None):
    ms = jnp.mean(x * x, axis=-1, keepdims=True)
    if anchor is not None:
        ms = ms + anchor
    y = x * lax.rsqrt(ms + EPS) * g
    return y * (1.0 + sc) + sh


def _mod_rows(mod_ref, k0):
    return (mod_ref[0, k0:k0 + 1, :], mod_ref[0, k0 + 1:k0 + 2, :], mod_ref[0, k0 + 2:k0 + 3, :])


def _dot_nt(a, b):
    return lax.dot_general(a, b, (((1,), (1,)), ((), ())), preferred_element_type=F32)


def _token_cols(ref, rows, features=slice(None)):
    tile = ref.shape[2]
    start, size = rows.start, rows.stop - rows.start
    assert start // tile == (start + size - 1) // tile
    return ref[start // tile, features, start % tile:start % tile + size]


def _ada_kernel(c_ref, w_ref, b_ref, o_ref):
    @pl.when(pl.program_id(0) == 0)
    def _():
        o_ref[...] = jnp.broadcast_to(b_ref[...], o_ref.shape)

    o_ref[...] += jnp.dot(_silu(c_ref[...]), w_ref[...], preferred_element_type=F32)


def _ada(c, w, b, tk=128):
    bsz, d = c.shape
    n = w.shape[1]
    return pl.pallas_call(
        _ada_kernel,
        out_shape=jax.ShapeDtypeStruct((bsz, n), F32),
        grid=(d // tk,),
        in_specs=[
            pl.BlockSpec((bsz, tk), lambda k: (0, k)),
            pl.BlockSpec((tk, n), lambda k: (k, 0)),
            _resident((1, n)),
        ],
        out_specs=pl.BlockSpec((bsz, n), lambda k: (0, 0)),
        compiler_params=_cparams(1),
        name="ada_mod",
    )(c, w, b)


def _ffn_kernel(*refs, k0, sub, ncast):
    x_ref, mod_ref, g_ref, w1_ref, w3_ref, w2_ref = refs[:6]
    cast_in = refs[6:6 + ncast]
    o_ref = refs[6 + ncast]
    cast_out = refs[7 + ncast:]
    sh, sc, gt = _mod_rows(mod_ref, k0)
    nsub = x_ref.shape[0] // sub

    def hidden(s):
        return _modnorm(x_ref[s * sub:(s + 1) * sub, :], g_ref[...], sc, sh).astype(BF16)

    h = hidden(0)
    for s in range(nsub):
        rows = slice(s * sub, (s + 1) * sub)
        a = jnp.dot(h, w1_ref[...], preferred_element_type=F32)
        h_next = hidden(s + 1) if s + 1 < nsub else None
        b = jnp.dot(h, w3_ref[...], preferred_element_type=F32)
        u = (_silu(a) * b).astype(BF16)
        d = jnp.dot(u, w2_ref[...], preferred_element_type=F32)
        o_ref[rows, :] = x_ref[rows, :] + (0.5 * gt) * d
        h = h_next
    for src, dst in zip(cast_in, cast_out):
        dst[...] = src[...].astype(dst.dtype)


def _cast_slab_spec(rows, cols, nsteps):
    hold = 1
    while (rows * hold) % (nsteps * BF16_SUBLANES):
        hold *= 2
    slab = rows * hold // nsteps
    return pl.BlockSpec((slab, cols), lambda i: (i // hold, 0))


def _ffn(x, mod, g, w1, w3, w2, *, k0, seq, casts=(), tm=1024, sub=256):
    t, d = x.shape
    f = w1.shape[1]
    tm = min(tm, seq)
    nsteps = t // tm
    cast_specs = [_cast_slab_spec(w.shape[0], w.shape[1], nsteps) for w in casts]
    outs = pl.pallas_call(
        functools.partial(_ffn_kernel, k0=k0, sub=sub, ncast=len(casts)),
        out_shape=[jax.ShapeDtypeStruct((t, d), F32)] + [jax.ShapeDtypeStruct(w.shape, BF16) for w in casts],
        grid=(nsteps,),
        in_specs=[
            pl.BlockSpec((tm, d), lambda i: (i, 0)),
            pl.BlockSpec((1, N_MOD, d), lambda i: (i * tm // seq, 0, 0)),
            _resident((1, d)),
            _resident((d, f)),
            _resident((d, f)),
            _resident((f, d)),
        ] + cast_specs,
        out_specs=[pl.BlockSpec((tm, d), lambda i: (i, 0))] + cast_specs,
        compiler_params=_cparams(1),
        name=f"ffn_{k0}",
    )(x, mod, g, w1, w3, w2, *casts)
    return outs[0], outs[1:]


def _rot_tables(c, s, dists):
    c_swapped = pltpu.roll(c, LANES // 2, 1)
    s_swapped = pltpu.roll(s, LANES // 2, 1)
    lane = lax.broadcasted_iota(jnp.int32, (1, LANES), 1)
    low = lane < LANES // 2
    tabs = []
    for dist, (ck, sk) in zip(dists, ((jnp.where(low, c, c_swapped), jnp.where(low, s, s_swapped)),
                                      (jnp.where(low, c_swapped, c), jnp.where(low, s_swapped, s)))):
        first = (lane & dist) == 0
        tabs.append((ck, jnp.where(first, -sk, 0.0), jnp.where(first, 0.0, sk), dist))
    return tabs


def _rot128(y, tabs):
    c, sa, sb, dist = tabs
    return y * c + pltpu.roll(y, LANES - dist, 1) * sa + pltpu.roll(y, dist, 1) * sb


def _proj_kernel(x_ref, mod_ref, g_ref, pos_ref, inv_ref, bcos_ref, bsin_ref, w_ref, gq_ref, gk_ref, gsum_ref,
                 qa_ref, ka_ref, vat_ref, qr_ref, kr_ref, krt_ref, vr_ref, gate_ref, mga_ref, mgr_ref,
                 cos_ref, sin_ref, *, sub):
    sh, sc, _ = _mod_rows(mod_ref, 3)
    nsub = x_ref.shape[0] // sub

    pos = pos_ref[...]
    inv = inv_ref[...]
    ang0 = pos[0:1, :].astype(F32) * inv
    c0, s0 = jnp.cos(ang0), jnp.sin(ang0)
    cos_ref[...] = c0 * bcos_ref[...] - s0 * bsin_ref[...]
    sin_ref[...] = s0 * bcos_ref[...] + c0 * bsin_ref[...]
    step = pos - pos[0:1, :] - lax.broadcasted_iota(jnp.int32, pos.shape, 0)

    @pl.when(jnp.max(jnp.where(step != 0, 1.0, 0.0)) > 0.0)
    def _():
        ang = pos.astype(F32) * inv
        cos_ref[...] = jnp.cos(ang)
        sin_ref[...] = jnp.sin(ang)

    def hidden(s, anchor=None):
        return _modnorm(x_ref[s * sub:(s + 1) * sub, :], g_ref[...], sc, sh, anchor).astype(BF16)

    def qk_norm_rot(zs, gain, tab, out_ref, rows):
        ss = jnp.dot((zs * zs).astype(BF16), gsum_ref[...], preferred_element_type=F32)
        y = zs * lax.rsqrt(ss * (1.0 / DA_HEAD_DIM) + EPS) * gain
        for j in range(DA_QK_W // LANES):
            sl = slice(j * LANES, (j + 1) * LANES)
            out_ref[rows, sl] = _rot128(y[:, sl], tab).astype(out_ref.dtype)

    d = x_ref.shape[1]
    h_next = hidden(0)
    z_next = jnp.dot(h_next, w_ref[:, :PROJ_W], preferred_element_type=F32)
    for s in range(nsub):
        rows = slice(s * sub, (s + 1) * sub)
        h_cur, z = h_next, z_next
        if s + 1 < nsub:
            h_next = hidden(s + 1, _zero_after(z[:, :1]))
            z_next = jnp.dot(h_next, w_ref[:, :PROJ_W], preferred_element_type=F32)
        tab_a, tab_r = _rot_tables(cos_ref[rows, :], sin_ref[rows, :], (DA_HEAD_DIM // 2, 1))
        o = 0
        qk_norm_rot(z[:, o:o + DA_QK_W], gq_ref[...] * (DA_HEAD_DIM ** -0.5 * LOG2E), tab_a, qa_ref, rows)
        o += DA_QK_W
        qk_norm_rot(z[:, o:o + DA_QK_W], gk_ref[...], tab_a, ka_ref, rows)
        o += DA_QK_W
        vat_ref[0, :, rows] = z[:, o:o + DA_V_W].T.astype(vat_ref.dtype)
        o += DA_V_W
        for j in range(RET_QK_W // LANES):
            qr_ref[rows, j * LANES:(j + 1) * LANES] = _rot128(
                z[:, o + j * LANES:o + (j + 1) * LANES], tab_r).astype(qr_ref.dtype)
        o += RET_QK_W
        for j in range(RET_QK_W // LANES):
            kj = _rot128(z[:, o + j * LANES:o + (j + 1) * LANES], tab_r) * (RET_QK_DIM ** -0.5)
            kr_ref[rows, j * LANES:(j + 1) * LANES] = kj.astype(kr_ref.dtype)
            krt_ref[0, j * LANES:(j + 1) * LANES, rows] = kj.T.astype(krt_ref.dtype)
        o += RET_QK_W
        vr_ref[rows, :] = z[:, o:o + RET_V_W].astype(vr_ref.dtype)
        o += RET_V_W
        gate_ref[rows, :] = z[:, o:o + RET_V_W].astype(gate_ref.dtype)
        mg = jnp.dot(h_cur, w_ref[:, PROJ_W:], preferred_element_type=F32)
        mga_ref[rows, :] = mg[:, :d].astype(mga_ref.dtype)
        mgr_ref[rows, :] = mg[:, d:].astype(mgr_ref.dtype)


def _proj(x, mod, g, pos, inv, w, gq, gk, gsum, *, seq, tm=1024, sub=256):
    t, d = x.shape
    n = w.shape[1]
    tm = min(tm, seq)
    base = lax.broadcasted_iota(F32, (tm, LANES), 0) * inv
    bcos, bsin = jnp.cos(base), jnp.sin(base)

    def rows(width):
        return jax.ShapeDtypeStruct((t, width), BF16), pl.BlockSpec((tm, width), lambda i: (i, 0))

    def cols(width):
        return (jax.ShapeDtypeStruct((t // tm, width, tm), BF16),
                pl.BlockSpec((1, width, tm), lambda i: (i, 0, 0)))

    outs = [rows(DA_QK_W), rows(DA_QK_W), cols(DA_V_W), rows(RET_QK_W), rows(RET_QK_W), cols(RET_QK_W),
            rows(RET_V_W), rows(RET_V_W), rows(d), rows(d)]
    return pl.pallas_call(
        functools.partial(_proj_kernel, sub=sub),
        out_shape=[s for s, _ in outs],
        grid=(t // tm,),
        in_specs=[
            pl.BlockSpec((tm, d), lambda i: (i, 0)),
            pl.BlockSpec((1, N_MOD, d), lambda i: (i * tm // seq, 0, 0)),
            _resident((1, d)),
            pl.BlockSpec((tm, 1), lambda i: (i, 0)),
            _resident((1, LANES)),
            _resident((tm, LANES)),
            _resident((tm, LANES)),
            _resident((d, n)),
            _resident((1, DA_QK_W)),
            _resident((1, DA_QK_W)),
            _resident((DA_QK_W, DA_QK_W)),
        ],
        out_specs=[b for _, b in outs],
        scratch_shapes=[pltpu.VMEM((tm, LANES), F32)] * 2,
        compiler_params=_cparams(1),
        name="mix_proj",
    )(x, mod, g, pos, inv, bcos, bsin, w, gq, gk, gsum)


REDUCE_SLAB_ROWS = 8
MAX_UNSHIFTED_LOG2_SCORE = 40.0


def _slab_reduce(x, reduce_fn):
    r, n = x.shape
    return reduce_fn(x.reshape(r // REDUCE_SLAB_ROWS, REDUCE_SLAB_ROWS, n), axis=0)


def _dattn_kernel(q_ref, k_ref, vt_ref, lq1_ref, lk1_ref, lq2_ref, lk2_ref, gs_ref, gq_ref, gk_ref, o_ref, *,
                  lam_init, sub, kchunk):
    seq = k_ref.shape[0]
    nsub = q_ref.shape[0] // sub
    ntile = (q_ref.shape[1] // LANES) * nsub
    nch = seq // kchunk
    lam = (jnp.exp(jnp.sum(lq1_ref[...] * lk1_ref[...], axis=-1, keepdims=True))
           - jnp.exp(jnp.sum(lq2_ref[...] * lk2_ref[...], axis=-1, keepdims=True)) + lam_init)
    lane = lax.broadcasted_iota(jnp.int32, (sub, LANES), 1)

    def head_cols(tile):
        return slice((tile // nsub) * LANES, (tile // nsub + 1) * LANES)

    def tile_rows(tile):
        return slice((tile % nsub) * sub, (tile % nsub + 1) * sub)

    def query_columns(tile):
        q = q_ref[tile_rows(tile), head_cols(tile)]
        zero = jnp.zeros_like(q)
        return jnp.concatenate([jnp.where(lane < DA_HEAD_DIM, q, zero),
                                jnp.where(lane >= DA_HEAD_DIM, q, zero)], axis=0)

    def finish(tile, ot, l):
        o = (ot[:, :sub] * (1.0 / l[:, :sub]) - ot[:, sub:] * (lam / l[:, sub:])).T
        ms = jnp.mean(o * o, axis=-1, keepdims=True)
        o_ref[tile_rows(tile), head_cols(tile)] = (
            o * lax.rsqrt(ms + EPS) * gs_ref[...] * (1.0 - lam_init)).astype(o_ref.dtype)

    def shifted_softmax_path():
        st = m = p = l = None
        for it in range(ntile + 2):
            run_a, run_b, run_c = it < ntile, 1 <= it <= ntile, it >= 2
            qq = query_columns(it) if run_a else None
            st_new, p_new = [], []
            mx = ls = ot = None
            for c in range(nch):
                rows = slice(c * kchunk, (c + 1) * kchunk)
                if run_a:
                    s_c = _dot_nt(k_ref[rows, head_cols(it)], qq)
                    st_new.append(s_c)
                    r = _slab_reduce(s_c, jnp.max)
                    mx = r if mx is None else jnp.maximum(mx, r)
                if run_b:
                    p_c = jnp.exp2(st[c] - m)
                    r = _slab_reduce(p_c, jnp.sum)
                    ls = r if ls is None else ls + r
                    p_new.append(p_c.astype(BF16))
                if run_c:
                    d = jnp.dot(_token_cols(vt_ref, rows, head_cols(it - 2)), p[c],
                                preferred_element_type=F32)
                    ot = d if ot is None else ot + d
            if run_c:
                finish(it - 2, ot, l)
            if run_b:
                p, l = p_new, jnp.sum(ls, axis=0, keepdims=True)
            if run_a:
                st, m = st_new, jnp.max(mx, axis=0, keepdims=True)

    def unshifted_softmax_path():
        p = l = None
        for it in range(ntile + 1):
            run_a, run_c = it < ntile, it >= 1
            qq = query_columns(it) if run_a else None
            p_new = []
            ls = ot = None
            for c in range(nch):
                rows = slice(c * kchunk, (c + 1) * kchunk)
                if run_a:
                    p_c = jnp.exp2(_dot_nt(k_ref[rows, head_cols(it)], qq))
                    r = _slab_reduce(p_c, jnp.sum)
                    ls = r if ls is None else ls + r
                    p_new.append(p_c.astype(BF16))
                if run_c:
                    d = jnp.dot(_token_cols(vt_ref, rows, head_cols(it - 1)), p[c],
                                preferred_element_type=F32)
                    ot = d if ot is None else ot + d
            if run_c:
                finish(it - 1, ot, l)
            if run_a:
                p, l = p_new, jnp.sum(ls, axis=0, keepdims=True)

    bound = (jnp.max(jnp.abs(gq_ref[...])) * jnp.max(jnp.abs(gk_ref[...]))
             * (1.02 * DA_HEAD_DIM * DA_HEAD_DIM ** -0.5 * LOG2E))
    small_scores = bound <= MAX_UNSHIFTED_LOG2_SCORE
    pl.when(small_scores)(unshifted_softmax_path)
    pl.when(jnp.logical_not(small_scores))(shifted_softmax_path)


def _dattn(qa, ka, vat, lq1, lk1, lq2, lk2, gs, gq, gk, *, bsz, seq, lam_init, tq=2048, sub=128, kchunk=512,
           heads_per_step=1):
    t = qa.shape[0]
    tq = min(tq, seq)
    kchunk = min(kchunk, seq)
    nq = seq // tq
    width = heads_per_step * LANES
    lam_spec = _resident((1, DA_HEAD_DIM))
    return pl.pallas_call(
        functools.partial(_dattn_kernel, lam_init=lam_init, sub=sub, kchunk=kchunk),
        out_shape=jax.ShapeDtypeStruct((t, DA_V_W), BF16),
        grid=(bsz, DA_HEADS // heads_per_step, nq),
        in_specs=[
            pl.BlockSpec((tq, width), lambda b, h, i: (b * nq + i, h)),
            pl.BlockSpec((seq, width), lambda b, h, i: (b, h)),
            pl.BlockSpec((seq // vat.shape[2], width, vat.shape[2]), lambda b, h, i: (b, h, 0)),
            lam_spec, lam_spec, lam_spec, lam_spec,
            _resident((1, DA_V_DIM)),
            lam_spec, lam_spec,
        ],
        out_specs=pl.BlockSpec((tq, width), lambda b, h, i: (b * nq + i, h)),
        compiler_params=_cparams(3),
        name="diff_attn",
    )(qa, ka, vat, lq1, lk1, lq2, lk2, gs, gq, gk)


def _ret_kernel(q_ref, k_ref, kt_ref, v_ref, gate_ref, df_ref, db_ref, gn_ref, o_ref,
                dmask_ref, zf_ref, xf_ref, zb_ref, xb_ref, *, chunk, seq):
    hh = pl.program_id(0)
    nchunk = q_ref.shape[0] // chunk
    per_seq = seq // chunk
    hl = lax.broadcasted_iota(jnp.int32, df_ref.shape, 1) == hh

    def head_log_gamma(ref):
        lg = jax.nn.log_sigmoid(ref[...])
        return jnp.sum(jnp.where(hl, lg, 0.0), axis=-1, keepdims=True)

    lgf = head_log_gamma(df_ref)
    lgb = head_log_gamma(db_ref)

    @pl.when(pl.program_id(1) == 0)
    def _():
        a = lax.broadcasted_iota(jnp.int32, (chunk, chunk), 0)
        b = lax.broadcasted_iota(jnp.int32, (chunk, chunk), 1)
        rel = (a - b).astype(F32)
        dec = jnp.exp(jnp.where(rel >= 0, lgf, -lgb) * rel)
        dmask_ref[...] = jnp.where(rel == 0, 2.0, dec)
        r = lax.broadcasted_iota(jnp.int32, (chunk, LANES), 0).astype(F32)
        zf_ref[...] = jnp.exp((chunk - 1.0 - r) * lgf)
        xf_ref[...] = jnp.exp((r + 1.0) * lgf)
        zb_ref[...] = jnp.exp(r * lgb)
        xb_ref[...] = jnp.exp((chunk - r) * lgb)

    def rows(n):
        return slice(n * chunk, (n + 1) * chunk)

    lane = lax.broadcasted_iota(jnp.int32, (chunk, LANES), 1)
    mine = (lane // RET_QK_DIM) == (hh % 2)

    def scores(n):
        qn = q_ref[rows(n), :]
        qn = jnp.where(mine, qn, jnp.zeros_like(qn))
        return qn, _dot_nt(qn, k_ref[rows(n), :])

    ahead = [scores(0)]
    kvf, kvb = [], []
    for n in range(nchunk):
        vn = v_ref[rows(n), :].astype(F32)
        ktn = _token_cols(kt_ref, rows(n))
        kvf.append(jnp.dot(ktn, (vn * zf_ref[...]).astype(BF16), preferred_element_type=F32))
        kvb.append(jnp.dot(ktn, (vn * zb_ref[...]).astype(BF16), preferred_element_type=F32))
    gcf = jnp.exp(chunk * lgf)
    gcb = jnp.exp(chunk * lgb)
    rf, rb = [None] * nchunk, [None] * nchunk
    for first in range(0, nchunk, per_seq):
        r = jnp.zeros((LANES, RET_V_DIM), F32)
        for n in range(first, first + per_seq):
            rf[n] = r
            r = r * gcf + kvf[n]
        r = jnp.zeros((LANES, RET_V_DIM), F32)
        for n in reversed(range(first, first + per_seq)):
            rb[n] = r
            r = r * gcb + kvb[n]

    for n in range(nchunk):
        if n + 1 < nchunk:
            ahead.append(scores(n + 1))
        qn, s = ahead[n]
        qf = qn.astype(F32)
        lhs = jnp.concatenate([(s * dmask_ref[...]).astype(BF16), (qf * xf_ref[...]).astype(BF16),
                               (qf * xb_ref[...]).astype(BF16)], axis=1)
        rhs = jnp.concatenate([v_ref[rows(n), :], rf[n].astype(BF16), rb[n].astype(BF16)], axis=0)
        y = jnp.dot(lhs, rhs, preferred_element_type=F32)
        ms = jnp.mean(y * y, axis=-1, keepdims=True)
        y = y * lax.rsqrt(ms + EPS) * gn_ref[...]
        o_ref[rows(n), :] = (y * _silu(gate_ref[rows(n), :].astype(F32))).astype(o_ref.dtype)


def _retention(qr, kr, krt, vr, gate, df, db, gn, *, bsz, seq, chunk=256, seqs_per_step=4):
    t = qr.shape[0]
    seqs_per_step = math.gcd(seqs_per_step, bsz)
    rows = seqs_per_step * seq
    return pl.pallas_call(
        functools.partial(_ret_kernel, chunk=chunk, seq=seq),
        out_shape=jax.ShapeDtypeStruct((t, RET_V_W), BF16),
        grid=(RET_HEADS, bsz // seqs_per_step),
        in_specs=[
            pl.BlockSpec((rows, LANES), lambda h, b: (b, h // 2)),
            pl.BlockSpec((rows, LANES), lambda h, b: (b, h // 2)),
            pl.BlockSpec((rows // krt.shape[2], LANES, krt.shape[2]), lambda h, b: (b, h // 2, 0)),
            pl.BlockSpec((rows, LANES), lambda h, b: (b, h)),
            pl.BlockSpec((rows, LANES), lambda h, b: (b, h)),
            _resident((1, RET_HEADS)),
            _resident((1, RET_HEADS)),
            _resident((1, RET_V_DIM)),
        ],
        out_specs=pl.BlockSpec((rows, LANES), lambda h, b: (b, h)),
        scratch_shapes=[pltpu.VMEM((chunk, chunk), F32)] + [pltpu.VMEM((chunk, LANES), F32)] * 4,
        compiler_params=_cparams(2),
        name="retention",
    )(qr, kr, krt, vr, gate, df, db, gn)


def _merge_kernel(x_ref, mod_ref, oa_ref, yr_ref, mga_ref, mgr_ref, bm_ref, wa_ref, wr_ref, wo_ref, o_ref, *, sub):
    _, _, gt = _mod_rows(mod_ref, 3)
    nsub = x_ref.shape[0] // sub

    def branch_projections(s):
        rows = slice(s * sub, (s + 1) * sub)
        return (jnp.dot(oa_ref[rows, :], wa_ref[...], preferred_element_type=F32),
                jnp.dot(yr_ref[rows, :], wr_ref[...], preferred_element_type=F32))

    ahead = branch_projections(0)
    for s in range(nsub):
        rows = slice(s * sub, (s + 1) * sub)
        pa, pr = ahead
        if s + 1 < nsub:
            ahead = branch_projections(s + 1)
        merged = (jax.nn.sigmoid(mga_ref[rows, :].astype(F32) + bm_ref[0:1, :]) * pa
                  + jax.nn.sigmoid(mgr_ref[rows, :].astype(F32) + bm_ref[1:2, :]) * pr)
        out = jnp.dot(merged.astype(BF16), wo_ref[...], preferred_element_type=F32)
        o_ref[rows, :] = x_ref[rows, :] + gt * out


def _merge(x, mod, oa, yr, mga, mgr, bm, wa, wr, wo, *, seq, tm=1024, sub=256):
    t, d = x.shape
    tm = min(tm, seq)
    return pl.pallas_call(
        functools.partial(_merge_kernel, sub=sub),
        out_shape=jax.ShapeDtypeStruct((t, d), F32),
        grid=(t // tm,),
        in_specs=[
            pl.BlockSpec((tm, d), lambda i: (i, 0)),
            pl.BlockSpec((1, N_MOD, d), lambda i: (i * tm // seq, 0, 0)),
            pl.BlockSpec((tm, DA_V_W), lambda i: (i, 0)),
            pl.BlockSpec((tm, RET_V_W), lambda i: (i, 0)),
            pl.BlockSpec((tm, d), lambda i: (i, 0)),
            pl.BlockSpec((tm, d), lambda i: (i, 0)),
            _resident((2, d)),
            _resident((DA_V_W, d)),
            _resident((RET_V_W, d)),
            _resident((d, d)),
        ],
        out_specs=pl.BlockSpec((tm, d), lambda i: (i, 0)),
        compiler_params=_cparams(1),
        name="mix_merge",
    )(x, mod, oa, yr, mga, mgr, bm, wa, wr, wo)


def _head_freqs(inv_half, repeat):
    return jnp.tile(jnp.repeat(inv_half, repeat), 2 // repeat)


def kernel(x, c, positions, w_ada, b_ada, norm_ffn1, ffn1_w1, ffn1_w3, ffn1_w2, norm_mix, w_in, b_merge,
           da_q_gain, da_k_gain, da_lambda_q1, da_lambda_k1, da_lambda_q2, da_lambda_k2, da_subln,
           ret_decay_f, ret_decay_b, ret_norm, w_branch_a, w_branch_r, w_out, norm_ffn2, ffn2_w1, ffn2_w3,
           ffn2_w2):
    bsz, seq, d = x.shape
    depth = w_ada.shape[0]
    t = bsz * seq
    xt = x.reshape(t, d)
    pos = positions.reshape(t, 1)

    inv = jnp.concatenate([
        _head_freqs(1.0 / (ROPE_THETA ** (jnp.arange(0, DA_HEAD_DIM, 2, dtype=F32) / DA_HEAD_DIM)), 1),
        _head_freqs(1.0 / (ROPE_THETA ** jnp.linspace(0.0, 1.0, RET_QK_DIM // 2, dtype=F32)), 2),
    ]).reshape(1, LANES)
    n_groups = DA_QK_W // DA_HEAD_DIM
    gsum = jnp.kron(jnp.eye(n_groups, dtype=F32), jnp.ones((DA_HEAD_DIM, DA_HEAD_DIM), F32)).astype(BF16)

    for l in range(depth):
        lam_init = 0.8 - 0.6 * math.exp(-0.3 * l)
        mod = _ada(c, w_ada[l], b_ada[l].reshape(1, -1)).reshape(bsz, N_MOD, d)

        xt, (w_in_b, wa_b, wr_b, wo_b, f2w1_b, f2w3_b, f2w2_b) = _ffn(
            xt, mod, norm_ffn1[l].reshape(1, d), ffn1_w1[l].astype(BF16), ffn1_w3[l].astype(BF16),
            ffn1_w2[l].astype(BF16), k0=0, seq=seq,
            casts=(w_in[l], w_branch_a[l], w_branch_r[l], w_out[l], ffn2_w1[l], ffn2_w3[l], ffn2_w2[l]))

        qa, ka, vat, qr, kr, krt, vr, gate, mga, mgr = _proj(
            xt, mod, norm_mix[l].reshape(1, d), pos, inv, w_in_b,
            jnp.tile(da_q_gain[l], n_groups).reshape(1, DA_QK_W),
            jnp.tile(da_k_gain[l], n_groups).reshape(1, DA_QK_W),
            gsum, seq=seq)

        oa = _dattn(qa, ka, vat, da_lambda_q1[l].reshape(1, -1), da_lambda_k1[l].reshape(1, -1),
                    da_lambda_q2[l].reshape(1, -1), da_lambda_k2[l].reshape(1, -1),
                    da_subln[l].reshape(1, -1), da_q_gain[l].reshape(1, -1), da_k_gain[l].reshape(1, -1),
                    bsz=bsz, seq=seq, lam_init=lam_init)
        yr = _retention(qr, kr, krt, vr, gate, ret_decay_f[l].reshape(1, -1), ret_decay_b[l].reshape(1, -1),
                        ret_norm[l].reshape(1, -1), bsz=bsz, seq=seq)

        xt = _merge(xt, mod, oa, yr, mga, mgr, b_merge[l], wa_b, wr_b, wo_b, seq=seq)

        xt, _ = _ffn(xt, mod, norm_ffn2[l].reshape(1, d), f2w1_b, f2w3_b, f2w2_b, k0=6, seq=seq)
    return xt.reshape(bsz, seq, d)
```

```python
import functools
import math

import jax
import jax.numpy as jnp
from jax import lax
from jax.experimental import pallas as pl
from jax.experimental.pallas import tpu as pltpu

F32 = jnp.float32
BF16 = jnp.bfloat16

DA_HEADS = 4
DA_HEAD_DIM = 64
DA_V_DIM = 2 * DA_HEAD_DIM
RET_HEADS = 4
RET_QK_DIM = 64
RET_V_DIM = 128
ROPE_THETA = 10000.0
EPS = 1e-6
N_MOD = 9
LOG2E = math.log2(math.e)

DA_QK_W = DA_HEADS * 2 * DA_HEAD_DIM
DA_V_W = DA_HEADS * DA_V_DIM
RET_QK_W = RET_HEADS * RET_QK_DIM
RET_V_W = RET_HEADS * RET_V_DIM
PROJ_W = 2 * DA_QK_W + DA_V_W + 2 * RET_QK_W + 2 * RET_V_W

LANES = 128
BF16_SUBLANES = 16
VMEM_LIMIT_BYTES = 56 * 1024 * 1024


def _cparams(n_axes):
    return pltpu.CompilerParams(
        dimension_semantics=("arbitrary",) * n_axes,
        vmem_limit_bytes=VMEM_LIMIT_BYTES,
    )


def _resident(shape):
    nd = len(shape)
    return pl.BlockSpec(shape, lambda *_: (0,) * nd, pipeline_mode=pl.Buffered(1))


def _silu(a):
    return a * jax.nn.sigmoid(a)


def _zero_after(v):
    return ((pltpu.bitcast(v, jnp.uint32) >> 16) >> 16).astype(F32)


def _modnorm(x, g, sc, sh, anchor=None):
    ms = jnp.mean(x * x, axis=-1, keepdims=True)
    if anchor is not None:
        ms = ms + anchor
    y = x * lax.rsqrt(ms + EPS) * g
    return y * (1.0 + sc) + sh


def _mod_rows(mod_ref, k0):
    return (mod_ref[0, k0:k0 + 1, :], mod_ref[0, k0 + 1:k0 + 2, :], mod_ref[0, k0 + 2:k0 + 3, :])


def _dot_nt(a, b):
    return lax.dot_general(a, b, (((1,), (1,)), ((), ())), preferred_element_type=F32)


def _token_cols(ref, rows, features=slice(None)):
    tile = ref.shape[2]
    start, size = rows.start, rows.stop - rows.start
    assert start // tile == (start + size - 1) // tile
    return ref[start // tile, features, start % tile:start % tile + size]


def _ada_kernel(c_ref, w_ref, b_ref, o_ref):
    @pl.when(pl.program_id(0) == 0)
    def _():
        o_ref[...] = jnp.broadcast_to(b_ref[...], o_ref.shape)

    o_ref[...] += jnp.dot(_silu(c_ref[...]), w_ref[...], preferred_element_type=F32)


def _ada(c, w, b, tk=128):
    bsz, d = c.shape
    n = w.shape[1]
    return pl.pallas_call(
        _ada_kernel,
        out_shape=jax.ShapeDtypeStruct((bsz, n), F32),
        grid=(d // tk,),
        in_specs=[
            pl.BlockSpec((bsz, tk), lambda k: (0, k)),
            pl.BlockSpec((tk, n), lambda k: (k, 0)),
            _resident((1, n)),
        ],
        out_specs=pl.BlockSpec((bsz, n), lambda k: (0, 0)),
        compiler_params=_cparams(1),
        name="ada_mod",
    )(c, w, b)


def _ffn_kernel(*refs, k0, sub, ncast):
    x_ref, mod_ref, g_ref, w1_ref, w3_ref, w2_ref = refs[:6]
    cast_in = refs[6:6 + ncast]
    o_ref = refs[6 + ncast]
    cast_out = refs[7 + ncast:]
    sh, sc, gt = _mod_rows(mod_ref, k0)
    nsub = x_ref.shape[0] // sub

    def hidden(s):
        return _modnorm(x_ref[s * sub:(s + 1) * sub, :], g_ref[...], sc, sh).astype(BF16)

    h = hidden(0)
    for s in range(nsub):
        rows = slice(s * sub, (s + 1) * sub)
        a = jnp.dot(h, w1_ref[...], preferred_element_type=F32)
        h_next = hidden(s + 1) if s + 1 < nsub else None
        b = jnp.dot(h, w3_ref[...], preferred_element_type=F32)
        u = (_silu(a) * b).astype(BF16)
        d = jnp.dot(u, w2_ref[...], preferred_element_type=F32)
        o_ref[rows, :] = x_ref[rows, :] + (0.5 * gt) * d
        h = h_next
    for src, dst in zip(cast_in, cast_out):
        dst[...] = src[...].astype(dst.dtype)


def _cast_slab_spec(rows, cols, nsteps):
    hold = 1
    while (rows * hold) % (nsteps * BF16_SUBLANES):
        hold *= 2
    slab = rows * hold // nsteps
    return pl.BlockSpec((slab, cols), lambda i: (i // hold, 0))


def _ffn(x, mod, g, w1, w3, w2, *, k0, seq, casts=(), tm=1024, sub=256):
    t, d = x.shape
    f = w1.shape[1]
    tm = min(tm, seq)
    nsteps = t // tm
    cast_specs = [_cast_slab_spec(w.shape[0], w.shape[1], nsteps) for w in casts]
    outs = pl.pallas_call(
        functools.partial(_ffn_kernel, k0=k0, sub=sub, ncast=len(casts)),
        out_shape=[jax.ShapeDtypeStruct((t, d), F32)] + [jax.ShapeDtypeStruct(w.shape, BF16) for w in casts],
        grid=(nsteps,),
        in_specs=[
            pl.BlockSpec((tm, d), lambda i: (i, 0)),
            pl.BlockSpec((1, N_MOD, d), lambda i: (i * tm // seq, 0, 0)),
            _resident((1, d)),
            _resident((d, f)),
            _resident((d, f)),
            _resident((f, d)),
        ] + cast_specs,
        out_specs=[pl.BlockSpec((tm, d), lambda i: (i, 0))] + cast_specs,
        compiler_params=_cparams(1),
        name=f"ffn_{k0}",
    )(x, mod, g, w1, w3, w2, *casts)
    return outs[0], outs[1:]


def _rot_tables(c, s, dists):
    c_swapped = pltpu.roll(c, LANES // 2, 1)
    s_swapped = pltpu.roll(s, LANES // 2, 1)
    lane = lax.broadcasted_iota(jnp.int32, (1, LANES), 1)
    low = lane < LANES // 2
    tabs = []
    for dist, (ck, sk) in zip(dists, ((jnp.where(low, c, c_swapped), jnp.where(low, s, s_swapped)),
                                      (jnp.where(low, c_swapped, c), jnp.where(low, s_swapped, s)))):
        first = (lane & dist) == 0
        tabs.append((ck, jnp.where(first, -sk, 0.0), jnp.where(first, 0.0, sk), dist))
    return tabs


def _rot128(y, tabs):
    c, sa, sb, dist = tabs
    return y * c + pltpu.roll(y, LANES - dist, 1) * sa + pltpu.roll(y, dist, 1) * sb


def _proj_kernel(x_ref, mod_ref, g_ref, pos_ref, inv_ref, bcos_ref, bsin_ref, w_ref, gq_ref, gk_ref, gsum_ref,
                 qa_ref, ka_ref, vat_ref, qr_ref, kr_ref, krt_ref, vr_ref, gate_ref, cos_ref, sin_ref, *, sub):
    sh, sc, _ = _mod_rows(mod_ref, 3)
    nsub = x_ref.shape[0] // sub

    pos = pos_ref[...]
    inv = inv_ref[...]
    ang0 = pos[0:1, :].astype(F32) * inv
    c0, s0 = jnp.cos(ang0), jnp.sin(ang0)
    cos_ref[...] = c0 * bcos_ref[...] - s0 * bsin_ref[...]
    sin_ref[...] = s0 * bcos_ref[...] + c0 * bsin_ref[...]
    step = pos - pos[0:1, :] - lax.broadcasted_iota(jnp.int32, pos.shape, 0)

    @pl.when(jnp.max(jnp.where(step != 0, 1.0, 0.0)) > 0.0)
    def _():
        ang = pos.astype(F32) * inv
        cos_ref[...] = jnp.cos(ang)
        sin_ref[...] = jnp.sin(ang)

    def hidden(s, anchor=None):
        return _modnorm(x_ref[s * sub:(s + 1) * sub, :], g_ref[...], sc, sh, anchor).astype(BF16)

    def qk_norm_rot(zs, gain, tab, out_ref, rows):
        ss = jnp.dot((zs * zs).astype(BF16), gsum_ref[...], preferred_element_type=F32)
        y = zs * lax.rsqrt(ss * (1.0 / DA_HEAD_DIM) + EPS) * gain
        for j in range(DA_QK_W // LANES):
            sl = slice(j * LANES, (j + 1) * LANES)
            out_ref[rows, sl] = _rot128(y[:, sl], tab).astype(out_ref.dtype)

    z_next = jnp.dot(hidden(0), w_ref[...], preferred_element_type=F32)
    for s in range(nsub):
        rows = slice(s * sub, (s + 1) * sub)
        z = z_next
        if s + 1 < nsub:
            z_next = jnp.dot(hidden(s + 1, _zero_after(z[:, :1])), w_ref[...], preferred_element_type=F32)
        tab_a, tab_r = _rot_tables(cos_ref[rows, :], sin_ref[rows, :], (DA_HEAD_DIM // 2, 1))
        o = 0
        qk_norm_rot(z[:, o:o + DA_QK_W], gq_ref[...] * (DA_HEAD_DIM ** -0.5 * LOG2E), tab_a, qa_ref, rows)
        o += DA_QK_W
        qk_norm_rot(z[:, o:o + DA_QK_W], gk_ref[...], tab_a, ka_ref, rows)
        o += DA_QK_W
        vat_ref[0, :, rows] = z[:, o:o + DA_V_W].T.astype(vat_ref.dtype)
        o += DA_V_W
        for j in range(RET_QK_W // LANES):
            qr_ref[rows, j * LANES:(j + 1) * LANES] = _rot128(
                z[:, o + j * LANES:o + (j + 1) * LANES], tab_r).astype(qr_ref.dtype)
        o += RET_QK_W
        for j in range(RET_QK_W // LANES):
            kj = _rot128(z[:, o + j * LANES:o + (j + 1) * LANES], tab_r) * (RET_QK_DIM ** -0.5)
            kr_ref[rows, j * LANES:(j + 1) * LANES] = kj.astype(kr_ref.dtype)
            krt_ref[0, j * LANES:(j + 1) * LANES, rows] = kj.T.astype(krt_ref.dtype)
        o += RET_QK_W
        vr_ref[rows, :] = z[:, o:o + RET_V_W].astype(vr_ref.dtype)
        o += RET_V_W
        gate_ref[rows, :] = z[:, o:o + RET_V_W].astype(gate_ref.dtype)


def _proj(x, mod, g, pos, inv, w, gq, gk, gsum, *, seq, tm=1024, sub=256):
    t, d = x.shape
    n = PROJ_W
    tm = min(tm, seq)
    base = lax.broadcasted_iota(F32, (tm, LANES), 0) * inv
    bcos, bsin = jnp.cos(base), jnp.sin(base)

    def rows(width):
        return jax.ShapeDtypeStruct((t, width), BF16), pl.BlockSpec((tm, width), lambda i: (i, 0))

    def cols(width):
        return (jax.ShapeDtypeStruct((t // tm, width, tm), BF16),
                pl.BlockSpec((1, width, tm), lambda i: (i, 0, 0)))

    outs = [rows(DA_QK_W), rows(DA_QK_W), cols(DA_V_W), rows(RET_QK_W), rows(RET_QK_W), cols(RET_QK_W),
            rows(RET_V_W), rows(RET_V_W)]
    return pl.pallas_call(
        functools.partial(_proj_kernel, sub=sub),
        out_shape=[s for s, _ in outs],
        grid=(t // tm,),
        in_specs=[
            pl.BlockSpec((tm, d), lambda i: (i, 0)),
            pl.BlockSpec((1, N_MOD, d), lambda i: (i * tm // seq, 0, 0)),
            _resident((1, d)),
            pl.BlockSpec((tm, 1), lambda i: (i, 0)),
            _resident((1, LANES)),
            _resident((tm, LANES)),
            _resident((tm, LANES)),
            _resident((d, n)),
            _resident((1, DA_QK_W)),
            _resident((1, DA_QK_W)),
            _resident((DA_QK_W, DA_QK_W)),
        ],
        out_specs=[b for _, b in outs],
        scratch_shapes=[pltpu.VMEM((tm, LANES), F32)] * 2,
        compiler_params=_cparams(1),
        name="mix_proj",
    )(x, mod, g, pos, inv, bcos, bsin, w, gq, gk, gsum)


REDUCE_SLAB_ROWS = 8
MAX_UNSHIFTED_LOG2_SCORE = 40.0


def _slab_reduce(x, reduce_fn):
    r, n = x.shape
    return reduce_fn(x.reshape(r // REDUCE_SLAB_ROWS, REDUCE_SLAB_ROWS, n), axis=0)


def _dattn_kernel(q_ref, k_ref, vt_ref, lq1_ref, lk1_ref, lq2_ref, lk2_ref, gs_ref, gq_ref, gk_ref, o_ref, *,
                  lam_init, sub, kchunk):
    seq = k_ref.shape[0]
    nsub = q_ref.shape[0] // sub
    ntile = (q_ref.shape[1] // LANES) * nsub
    nch = seq // kchunk
    lam = (jnp.exp(jnp.sum(lq1_ref[...] * lk1_ref[...], axis=-1, keepdims=True))
           - jnp.exp(jnp.sum(lq2_ref[...] * lk2_ref[...], axis=-1, keepdims=True)) + lam_init)
    lane = lax.broadcasted_iota(jnp.int32, (sub, LANES), 1)

    def head_cols(tile):
        return slice((tile // nsub) * LANES, (tile // nsub + 1) * LANES)

    def tile_rows(tile):
        return slice((tile % nsub) * sub, (tile % nsub + 1) * sub)

    def query_columns(tile):
        q = q_ref[tile_rows(tile), head_cols(tile)]
        zero = jnp.zeros_like(q)
        return jnp.concatenate([jnp.where(lane < DA_HEAD_DIM, q, zero),
                                jnp.where(lane >= DA_HEAD_DIM, q, zero)], axis=0)

    def finish(tile, ot, l):
        o = (ot[:, :sub] * (1.0 / l[:, :sub]) - ot[:, sub:] * (lam / l[:, sub:])).T
        ms = jnp.mean(o * o, axis=-1, keepdims=True)
        o_ref[tile_rows(tile), head_cols(tile)] = (
            o * lax.rsqrt(ms + EPS) * gs_ref[...] * (1.0 - lam_init)).astype(o_ref.dtype)

    def shifted_softmax_path():
        st = m = p = l = None
        for it in range(ntile + 2):
            run_a, run_b, run_c = it < ntile, 1 <= it <= ntile, it >= 2
            qq = query_columns(it) if run_a else None
            st_new, p_new = [], []
            mx = ls = ot = None
            for c in range(nch):
                rows = slice(c * kchunk, (c + 1) * kchunk)
                if run_a:
                    s_c = _dot_nt(k_ref[rows, head_cols(it)], qq)
                    st_new.append(s_c)
                    r = _slab_reduce(s_c, jnp.max)
                    mx = r if mx is None else jnp.maximum(mx, r)
                if run_b:
                    p_c = jnp.exp2(st[c] - m)
                    r = _slab_reduce(p_c, jnp.sum)
                    ls = r if ls is None else ls + r
                    p_new.append(p_c.astype(BF16))
                if run_c:
                    d = jnp.dot(_token_cols(vt_ref, rows, head_cols(it - 2)), p[c],
                                preferred_element_type=F32)
                    ot = d if ot is None else ot + d
            if run_c:
                finish(it - 2, ot, l)
            if run_b:
                p, l = p_new, jnp.sum(ls, axis=0, keepdims=True)
            if run_a:
                st, m = st_new, jnp.max(mx, axis=0, keepdims=True)

    def unshifted_softmax_path():
        p = l = None
        for it in range(ntile + 1):
            run_a, run_c = it < ntile, it >= 1
            qq = query_columns(it) if run_a else None
            p_new = []
            ls = ot = None
            for c in range(nch):
                rows = slice(c * kchunk, (c + 1) * kchunk)
                if run_a:
                    p_c = jnp.exp2(_dot_nt(k_ref[rows, head_cols(it)], qq))
                    r = _slab_reduce(p_c, jnp.sum)
                    ls = r if ls is None else ls + r
                    p_new.append(p_c.astype(BF16))
                if run_c:
                    d = jnp.dot(_token_cols(vt_ref, rows, head_cols(it - 1)), p[c],
                                preferred_element_type=F32)
                    ot = d if ot is None else ot + d
            if run_c:
                finish(it - 1, ot, l)
            if run_a:
                p, l = p_new, jnp.sum(ls, axis=0, keepdims=True)

    bound = (jnp.max(jnp.abs(gq_ref[...])) * jnp.max(jnp.abs(gk_ref[...]))
             * (1.02 * DA_HEAD_DIM * DA_HEAD_DIM ** -0.5 * LOG2E))
    small_scores = bound <= MAX_UNSHIFTED_LOG2_SCORE
    pl.when(small_scores)(unshifted_softmax_path)
    pl.when(jnp.logical_not(small_scores))(shifted_softmax_path)


def _dattn(qa, ka, vat, lq1, lk1, lq2, lk2, gs, gq, gk, *, bsz, seq, lam_init, tq=2048, sub=128, kchunk=512,
           heads_per_step=1):
    t = qa.shape[0]
    tq = min(tq, seq)
    kchunk = min(kchunk, seq)
    nq = seq // tq
    width = heads_per_step * LANES
    lam_spec = _resident((1, DA_HEAD_DIM))
    return pl.pallas_call(
        functools.partial(_dattn_kernel, lam_init=lam_init, sub=sub, kchunk=kchunk),
        out_shape=jax.ShapeDtypeStruct((t, DA_V_W), BF16),
        grid=(bsz, DA_HEADS // heads_per_step, nq),
        in_specs=[
            pl.BlockSpec((tq, width), lambda b, h, i: (b * nq + i, h)),
            pl.BlockSpec((seq, width), lambda b, h, i: (b, h)),
            pl.BlockSpec((seq // vat.shape[2], width, vat.shape[2]), lambda b, h, i: (b, h, 0)),
            lam_spec, lam_spec, lam_spec, lam_spec,
            _resident((1, DA_V_DIM)),
            lam_spec, lam_spec,
        ],
        out_specs=pl.BlockSpec((tq, width), lambda b, h, i: (b * nq + i, h)),
        compiler_params=_cparams(3),
        name="diff_attn",
    )(qa, ka, vat, lq1, lk1, lq2, lk2, gs, gq, gk)


def _ret_kernel(q_ref, k_ref, kt_ref, v_ref, df_ref, db_ref, gn_ref, o_ref,
                dmask_ref, zf_ref, xf_ref, zb_ref, xb_ref, *, chunk, seq):
    hh = pl.program_id(0)
    nchunk = q_ref.shape[0] // chunk
    per_seq = seq // chunk
    hl = lax.broadcasted_iota(jnp.int32, df_ref.shape, 1) == hh

    def head_log_gamma(ref):
        lg = jax.nn.log_sigmoid(ref[...])
        return jnp.sum(jnp.where(hl, lg, 0.0), axis=-1, keepdims=True)

    lgf = head_log_gamma(df_ref)
    lgb = head_log_gamma(db_ref)

    @pl.when(pl.program_id(1) == 0)
    def _():
        a = lax.broadcasted_iota(jnp.int32, (chunk, chunk), 0)
        b = lax.broadcasted_iota(jnp.int32, (chunk, chunk), 1)
        rel = (a - b).astype(F32)
        dec = jnp.exp(jnp.where(rel >= 0, lgf, -lgb) * rel)
        dmask_ref[...] = jnp.where(rel == 0, 2.0, dec)
        r = lax.broadcasted_iota(jnp.int32, (chunk, LANES), 0).astype(F32)
        zf_ref[...] = jnp.exp((chunk - 1.0 - r) * lgf)
        xf_ref[...] = jnp.exp((r + 1.0) * lgf)
        zb_ref[...] = jnp.exp(r * lgb)
        xb_ref[...] = jnp.exp((chunk - r) * lgb)

    def rows(n):
        return slice(n * chunk, (n + 1) * chunk)

    lane = lax.broadcasted_iota(jnp.int32, (chunk, LANES), 1)
    mine = (lane // RET_QK_DIM) == (hh % 2)

    def scores(n):
        qn = q_ref[rows(n), :]
        qn = jnp.where(mine, qn, jnp.zeros_like(qn))
        return qn, _dot_nt(qn, k_ref[rows(n), :])

    ahead = [scores(0)]
    kvf, kvb = [], []
    for n in range(nchunk):
        vn = v_ref[rows(n), :].astype(F32)
        ktn = _token_cols(kt_ref, rows(n))
        kvf.append(jnp.dot(ktn, (vn * zf_ref[...]).astype(BF16), preferred_element_type=F32))
        kvb.append(jnp.dot(ktn, (vn * zb_ref[...]).astype(BF16), preferred_element_type=F32))
    gcf = jnp.exp(chunk * lgf)
    gcb = jnp.exp(chunk * lgb)
    rf, rb = [None] * nchunk, [None] * nchunk
    for first in range(0, nchunk, per_seq):
        r = jnp.zeros((LANES, RET_V_DIM), F32)
        for n in range(first, first + per_seq):
            rf[n] = r
            r = r * gcf + kvf[n]
        r = jnp.zeros((LANES, RET_V_DIM), F32)
        for n in reversed(range(first, first + per_seq)):
            rb[n] = r
            r = r * gcb + kvb[n]

    for n in range(nchunk):
        if n + 1 < nchunk:
            ahead.append(scores(n + 1))
        qn, s = ahead[n]
        qf = qn.astype(F32)
        lhs = jnp.concatenate([(s * dmask_ref[...]).astype(BF16), (qf * xf_ref[...]).astype(BF16),
                               (qf * xb_ref[...]).astype(BF16)], axis=1)
        rhs = jnp.concatenate([v_ref[rows(n), :], rf[n].astype(BF16), rb[n].astype(BF16)], axis=0)
        y = jnp.dot(lhs, rhs, preferred_element_type=F32)
        ms = jnp.mean(y * y, axis=-1, keepdims=True)
        o_ref[rows(n), :] = (y * lax.rsqrt(ms + EPS) * gn_ref[...]).astype(o_ref.dtype)


def _retention(qr, kr, krt, vr, df, db, gn, *, bsz, seq, chunk=256, seqs_per_step=4):
    t = qr.shape[0]
    seqs_per_step = math.gcd(seqs_per_step, bsz)
    rows = seqs_per_step * seq
    return pl.pallas_call(
        functools.partial(_ret_kernel, chunk=chunk, seq=seq),
        out_shape=jax.ShapeDtypeStruct((t, RET_V_W), BF16),
        grid=(RET_HEADS, bsz // seqs_per_step),
        in_specs=[
            pl.BlockSpec((rows, LANES), lambda h, b: (b, h // 2)),
            pl.BlockSpec((rows, LANES), lambda h, b: (b, h // 2)),
            pl.BlockSpec((rows // krt.shape[2], LANES, krt.shape[2]), lambda h, b: (b, h // 2, 0)),
            pl.BlockSpec((rows, LANES), lambda h, b: (b, h)),
            _resident((1, RET_HEADS)),
            _resident((1, RET_HEADS)),
            _resident((1, RET_V_DIM)),
        ],
        out_specs=pl.BlockSpec((rows, LANES), lambda h, b: (b, h)),
        scratch_shapes=[pltpu.VMEM((chunk, chunk), F32)] + [pltpu.VMEM((chunk, LANES), F32)] * 4,
        compiler_params=_cparams(2),
        name="retention",
    )(qr, kr, krt, vr, df, db, gn)


def _merge_kernel(x_ref, mod_ref, g_ref, oa_ref, yr_ref, rg_ref, wma_ref, wmr_ref, bm_ref, wa_ref, wr_ref, wo_ref,
                  o_ref, *, sub):
    sh, sc, gt = _mod_rows(mod_ref, 3)
    nsub = x_ref.shape[0] // sub

    def input_matmuls(s):
        rows = slice(s * sub, (s + 1) * sub)
        h = _modnorm(x_ref[rows, :], g_ref[...], sc, sh).astype(BF16)
        yg = (yr_ref[rows, :].astype(F32) * _silu(rg_ref[rows, :].astype(F32))).astype(BF16)
        return (jnp.dot(h, wma_ref[...], preferred_element_type=F32),
                jnp.dot(h, wmr_ref[...], preferred_element_type=F32),
                jnp.dot(oa_ref[rows, :], wa_ref[...], preferred_element_type=F32),
                jnp.dot(yg, wr_ref[...], preferred_element_type=F32))

    ahead = input_matmuls(0)
    for s in range(nsub):
        rows = slice(s * sub, (s + 1) * sub)
        ga, gr, pa, pr = ahead
        if s + 1 < nsub:
            ahead = input_matmuls(s + 1)
        merged = jax.nn.sigmoid(ga + bm_ref[0:1, :]) * pa + jax.nn.sigmoid(gr + bm_ref[1:2, :]) * pr
        out = jnp.dot(merged.astype(BF16), wo_ref[...], preferred_element_type=F32)
        o_ref[rows, :] = x_ref[rows, :] + gt * out


def _merge(x, mod, g, oa, yr, rg, w_in, bm, wa, wr, wo, *, seq, tm=1024, sub=256):
    t, d = x.shape
    tm = min(tm, seq)
    gate_blk = PROJ_W // d

    def gate_spec(j):
        return pl.BlockSpec((d, d), lambda i: (0, gate_blk + j), pipeline_mode=pl.Buffered(1))

    return pl.pallas_call(
        functools.partial(_merge_kernel, sub=sub),
        out_shape=jax.ShapeDtypeStruct((t, d), F32),
        grid=(t // tm,),
        in_specs=[
            pl.BlockSpec((tm, d), lambda i: (i, 0)),
            pl.BlockSpec((1, N_MOD, d), lambda i: (i * tm // seq, 0, 0)),
            _resident((1, d)),
            pl.BlockSpec((tm, DA_V_W), lambda i: (i, 0)),
            pl.BlockSpec((tm, RET_V_W), lambda i: (i, 0)),
            pl.BlockSpec((tm, RET_V_W), lambda i: (i, 0)),
            gate_spec(0),
            gate_spec(1),
            _resident((2, d)),
            _resident((DA_V_W, d)),
            _resident((RET_V_W, d)),
            _resident((d, d)),
        ],
        out_specs=pl.BlockSpec((tm, d), lambda i: (i, 0)),
        compiler_params=_cparams(1),
        name="mix_merge",
    )(x, mod, g, oa, yr, rg, w_in, w_in, bm, wa, wr, wo)


def _head_freqs(inv_half, repeat):
    return jnp.tile(jnp.repeat(inv_half, repeat), 2 // repeat)


def kernel(x, c, positions, w_ada, b_ada, norm_ffn1, ffn1_w1, ffn1_w3, ffn1_w2, norm_mix, w_in, b_merge,
           da_q_gain, da_k_gain, da_lambda_q1, da_lambda_k1, da_lambda_q2, da_lambda_k2, da_subln,
           ret_decay_f, ret_decay_b, ret_norm, w_branch_a, w_branch_r, w_out, norm_ffn2, ffn2_w1, ffn2_w3,
           ffn2_w2):
    bsz, seq, d = x.shape
    depth = w_ada.shape[0]
    t = bsz * seq
    xt = x.reshape(t, d)
    pos = positions.reshape(t, 1)

    inv = jnp.concatenate([
        _head_freqs(1.0 / (ROPE_THETA ** (jnp.arange(0, DA_HEAD_DIM, 2, dtype=F32) / DA_HEAD_DIM)), 1),
        _head_freqs(1.0 / (ROPE_THETA ** jnp.linspace(0.0, 1.0, RET_QK_DIM // 2, dtype=F32)), 2),
    ]).reshape(1, LANES)
    n_groups = DA_QK_W // DA_HEAD_DIM
    gsum = jnp.kron(jnp.eye(n_groups, dtype=F32), jnp.ones((DA_HEAD_DIM, DA_HEAD_DIM), F32)).astype(BF16)

    for l in range(depth):
        lam_init = 0.8 - 0.6 * math.exp(-0.3 * l)
        mod = _ada(c, w_ada[l], b_ada[l].reshape(1, -1)).reshape(bsz, N_MOD, d)

        xt, (w_in_b, wa_b, wr_b, wo_b, f2w1_b, f2w3_b, f2w2_b) = _ffn(
            xt, mod, norm_ffn1[l].reshape(1, d), ffn1_w1[l].astype(BF16), ffn1_w3[l].astype(BF16),
            ffn1_w2[l].astype(BF16), k0=0, seq=seq,
            casts=(w_in[l], w_branch_a[l], w_branch_r[l], w_out[l], ffn2_w1[l], ffn2_w3[l], ffn2_w2[l]))

        qa, ka, vat, qr, kr, krt, vr, gate = _proj(
            xt, mod, norm_mix[l].reshape(1, d), pos, inv, w_in_b,
            jnp.tile(da_q_gain[l], n_groups).reshape(1, DA_QK_W),
            jnp.tile(da_k_gain[l], n_groups).reshape(1, DA_QK_W),
            gsum, seq=seq)

        oa = _dattn(qa, ka, vat, da_lambda_q1[l].reshape(1, -1), da_lambda_k1[l].reshape(1, -1),
                    da_lambda_q2[l].reshape(1, -1), da_lambda_k2[l].reshape(1, -1),
                    da_subln[l].reshape(1, -1), da_q_gain[l].reshape(1, -1), da_k_gain[l].reshape(1, -1),
                    bsz=bsz, seq=seq, lam_init=lam_init)
        yr = _retention(qr, kr, krt, vr, ret_decay_f[l].reshape(1, -1), ret_decay_b[l].reshape(1, -1),
                        ret_norm[l].reshape(1, -1), bsz=bsz, seq=seq)

        xt = _merge(xt, mod, norm_mix[l].reshape(1, d), oa, yr, gate, w_in_b, b_merge[l], wa_b, wr_b, wo_b, seq=seq)

        xt, _ = _ffn(xt, mod, norm_ffn2[l].reshape(1, d), f2w1_b, f2w3_b, f2w2_b, k0=6, seq=seq)
    return xt.reshape(bsz, seq, d)
```
